```python
import math
import jax
import jax.numpy as jnp
from jax import lax
import numpy as np

D_MODEL = 1024
BATCH = 8
SEQ = 2048
DEPTH = 4
DEC_BATCH = 128
DEC_SEQ = 8
PAST_LEN = 16384
PAGE_SIZE = 128

N_META = 16
N_MIXERS = 3
CHUNK = 64
ALPHA = (2 * DEPTH) ** 0.25
BETA = (8 * DEPTH) ** -0.25
LN_EPS = 1e-5
D_FF = ((8 * D_MODEL // 3 + 127) // 128) * 128
HG_EXPAND = 128
HG_HEADS = D_MODEL // HG_EXPAND
HG_DK = HG_EXPAND
HG_DV = D_MODEL // HG_HEADS
RET_HEADS = 4
RET_DK = D_MODEL // RET_HEADS
RET_DV = 2 * D_MODEL // RET_HEADS
ROPE_BASE = 10000.0
M_DI = 2 * D_MODEL
M_HEADDIM = 64
M_HEADS = M_DI // M_HEADDIM
M_GROUPS = 8
M_DSTATE = 128
M_CONV = 4
M_CONV_DIM = M_DI + 2 * M_GROUPS * M_DSTATE
N_HG_LAYERS = len(range(0, DEPTH, N_MIXERS))
N_RET_LAYERS = len(range(1, DEPTH, N_MIXERS))
N_SSM_LAYERS = len(range(2, DEPTH, N_MIXERS))

kernel_name = 'hybrid_hgrn2_retnet_mamba2_macaron_deepnorm_step'

F32 = jnp.float32


def layer_norm(x, g, b):
    xf = x.astype(F32)
    mu = jnp.mean(xf, axis=-1, keepdims=True)
    var = jnp.mean(jnp.square(xf - mu), axis=-1, keepdims=True)
    return ((xf - mu) * lax.rsqrt(var + LN_EPS) * g.astype(F32) + b.astype(F32)).astype(x.dtype)


def head_layer_norm(o, g):
    mu = jnp.mean(o, axis=-1, keepdims=True)
    var = jnp.mean(jnp.square(o - mu), axis=-1, keepdims=True)
    return (o - mu) * lax.rsqrt(var + LN_EPS) * g.astype(F32).reshape(o.shape[2:])


def head_rms_norm(o, g):
    ms = jnp.mean(jnp.square(o), axis=-1, keepdims=True)
    return o * lax.rsqrt(ms + LN_EPS) * g.astype(F32).reshape(o.shape[2:])


def swiglu(x, w_gate, w_up, w_down):
    return (jax.nn.silu(x @ w_gate) * (x @ w_up)) @ w_down


def rotary(x, pos):
    half = x.shape[-1] // 2
    inv_freq = ROPE_BASE ** (-jnp.arange(half, dtype=F32) / half)
    ang = pos.astype(F32)[:, None] * inv_freq[None, :]
    cos = jnp.cos(ang)[None, :, None, :]
    sin = jnp.sin(ang)[None, :, None, :]
    xf = x.astype(F32)
    x1, x2 = xf[..., :half], xf[..., half:]
    return jnp.concatenate([x1 * cos - x2 * sin, x1 * sin + x2 * cos], axis=-1).astype(x.dtype)


def _chunk_scan(q, k, v, log_a, s0):
    bsz, t = q.shape[:2]
    c = math.gcd(t, CHUNK)
    n = t // c

    def blocks(a):
        a = a.astype(F32)
        return a.reshape((bsz, n, c) + a.shape[2:]).swapaxes(0, 1)

    mask = jnp.tril(jnp.ones((c, c), dtype=bool))
    vector_decay = log_a.shape[-1] > 1

    def step(s, blk):
        qc, kc, vc, gc = blk
        g = jnp.cumsum(gc, axis=1)
        if vector_decay:
            diff = g[:, :, None] - g[:, None]
            dec = jnp.exp(jnp.where(mask[None, :, :, None, None], diff, -jnp.inf))
            att = jnp.einsum('bihk,bijhk,bjhk->bijh', qc, dec, kc)
        else:
            gs = g[..., 0]
            diff = gs[:, :, None] - gs[:, None]
            dec = jnp.exp(jnp.where(mask[None, :, :, None], diff, -jnp.inf))
            att = jnp.einsum('bihk,bjhk->bijh', qc, kc) * dec
        o = jnp.einsum('bijh,bjhv->bihv', att, vc) + jnp.einsum('bihk,bhkv->bihv', qc * jnp.exp(g), s)
        g_last = g[:, -1]
        k_dec = kc * jnp.exp(g_last[:, None] - g)
        s = jnp.exp(g_last)[..., None] * s + jnp.einsum('bjhk,bjhv->bhkv', k_dec, vc)
        return s, o

    s, o = lax.scan(step, s0.astype(F32), (blocks(q), blocks(k), blocks(v), blocks(log_a)))
    return o.swapaxes(0, 1).reshape((bsz, t) + o.shape[3:]), s


def recur(q, k, v, log_a, s0, n_lead):
    t = q.shape[1]
    outs = []
    s = s0
    for lo, hi in ((0, n_lead), (n_lead, t)):
        if hi > lo:
            o, s = _chunk_scan(q[:, lo:hi], k[:, lo:hi], v[:, lo:hi], log_a[:, lo:hi], s)
            outs.append(o)
    return jnp.concatenate(outs, axis=1), s


def causal_depthwise_conv(xcat, w, b):
    y = lax.conv_general_dilated(
        xcat, w.astype(xcat.dtype)[:, None, :], window_strides=(1,), padding='VALID',
        dimension_numbers=('NWC', 'WIO', 'NWC'), feature_group_count=xcat.shape[-1])
    return y + b.astype(xcat.dtype)


def hgrn2_mixer(x, s0, w_in, norm_g, w_o, lb, n_lead):
    bsz, t, _ = x.shape
    fw = HG_HEADS * HG_DK
    vw = HG_HEADS * HG_DV
    proj = x @ w_in
    q = jax.nn.silu(proj[..., :fw]).reshape(bsz, t, HG_HEADS, HG_DK)
    z = proj[..., fw:2 * fw].astype(F32)
    inp = proj[..., 2 * fw:2 * fw + vw].reshape(bsz, t, HG_HEADS, HG_DV)
    gate = proj[..., 2 * fw + vw:]
    f = lb + (1.0 - lb) * jax.nn.sigmoid(z)
    k = ((1.0 - lb) * jax.nn.sigmoid(-z)).reshape(bsz, t, HG_HEADS, HG_DK)
    log_f = jnp.log(f).reshape(bsz, t, HG_HEADS, HG_DK)
    o, s = recur(q, k, inp, log_f, s0, n_lead)
    o = head_rms_norm(o, norm_g).reshape(bsz, t, vw).astype(x.dtype) * jax.nn.silu(gate)
    return o @ w_o, s


def retention_mixer(x, s0, w_in, norm_g, w_o, pos, n_lead):
    bsz, t, _ = x.shape
    qk = RET_HEADS * RET_DK
    vw = RET_HEADS * RET_DV
    proj = x @ w_in
    q = rotary(proj[..., :qk].reshape(bsz, t, RET_HEADS, RET_DK), pos)
    k = rotary(proj[..., qk:2 * qk].reshape(bsz, t, RET_HEADS, RET_DK), pos) * (RET_DK ** -0.5)
    v = proj[..., 2 * qk:2 * qk + vw].reshape(bsz, t, RET_HEADS, RET_DV)
    gate = proj[..., 2 * qk + vw:]
    log_gamma = jnp.log(1.0 - 2.0 ** (-5.0 - jnp.arange(RET_HEADS, dtype=F32)))
    log_a = jnp.broadcast_to(log_gamma[None, None, :, None], (bsz, t, RET_HEADS, 1))
    o, s = recur(q, k, v, log_a, s0, n_lead)
    o = head_layer_norm(o, norm_g).reshape(bsz, t, vw).astype(x.dtype) * jax.nn.silu(gate)
    return o @ w_o, s


def mamba2_mixer(x, s0, conv0, w_in, conv_w, conv_b, dt_bias, a_log, d_skip, norm_g, w_o, n_lead):
    bsz, t, _ = x.shape
    proj = x @ w_in
    z = proj[..., :M_DI]
    xbc = proj[..., M_DI:M_DI + M_CONV_DIM]
    dt_raw = proj[..., M_DI + M_CONV_DIM:]
    xcat = jnp.concatenate([conv0.astype(xbc.dtype), xbc], axis=1)
    new_conv = xcat[:, -(M_CONV - 1):]
    xbc = jax.nn.silu(causal_depthwise_conv(xcat, conv_w, conv_b))
    gn = M_GROUPS * M_DSTATE
    rep = M_HEADS // M_GROUPS
    xs = xbc[..., :M_DI].reshape(bsz, t, M_HEADS, M_HEADDIM)
    bm = jnp.repeat(xbc[..., M_DI:M_DI + gn].reshape(bsz, t, M_GROUPS, M_DSTATE), rep, axis=2)
    cm = jnp.repeat(xbc[..., M_DI + gn:].reshape(bsz, t, M_GROUPS, M_DSTATE), rep, axis=2)
    dt = jax.nn.softplus(dt_raw.astype(F32) + dt_bias.astype(F32))
    log_a = (dt * -jnp.exp(a_log.astype(F32)))[..., None]
    xf = xs.astype(F32)
    o, s = recur(cm, bm, xf * dt[..., None], log_a, s0, n_lead)
    y = o + d_skip.astype(F32)[:, None] * xf
    y = y.reshape(bsz, t, M_DI) * jax.nn.silu(z.astype(F32))
    yg = y.reshape(bsz, t, M_GROUPS, M_DI // M_GROUPS)
    yg = yg * lax.rsqrt(jnp.mean(jnp.square(yg), axis=-1, keepdims=True) + LN_EPS)
    y = (yg.reshape(bsz, t, M_DI) * norm_g.astype(F32)).astype(x.dtype)
    return y @ w_o, s, new_conv


def run_trunk(h, st_hg, st_ret, st_ssm, st_conv, pos, n_lead, p):
    lb_all = jnp.cumsum(jax.nn.softmax(p['hg_lb_logits'].astype(F32), axis=0), axis=0)
    lb_all = lb_all - lb_all[0]
    new_hg, new_ret, new_ssm, new_conv = [], [], [], []
    for i in range(DEPTH):
        h = layer_norm(ALPHA * h + 0.5 * swiglu(h, p['ffn_w_gate'][i, 0], p['ffn_w_up'][i, 0], p['ffn_w_down'][i, 0]),
                       p['ln_g'][i, 0], p['ln_b'][i, 0])
        kind, j = i % N_MIXERS, i // N_MIXERS
        if kind == 0:
            m, s = hgrn2_mixer(h, st_hg[j], p['hg_w_in'][j], p['hg_norm_g'][j], p['hg_w_o'][j], lb_all[i], n_lead)
            new_hg.append(s)
        elif kind == 1:
            m, s = retention_mixer(h, st_ret[j], p['ret_w_in'][j], p['ret_norm_g'][j], p['ret_w_o'][j], pos, n_lead)
            new_ret.append(s)
        else:
            m, s, c = mamba2_mixer(h, st_ssm[j], st_conv[j], p['m_w_in'][j], p['m_conv_w'][j], p['m_conv_b'][j],
                                   p['m_dt_bias'][j], p['m_a_log'][j], p['m_d'][j], p['m_norm_g'][j], p['m_w_o'][j], n_lead)
            new_ssm.append(s)
            new_conv.append(c)
        h = layer_norm(ALPHA * h + m, p['ln_g'][i, 1], p['ln_b'][i, 1])
        h = layer_norm(ALPHA * h + 0.5 * swiglu(h, p['ffn_w_gate'][i, 1], p['ffn_w_up'][i, 1], p['ffn_w_down'][i, 1]),
                       p['ln_g'][i, 2], p['ln_b'][i, 2])
    return h, jnp.stack(new_hg), jnp.stack(new_ret), jnp.stack(new_ssm), jnp.stack(new_conv)


def setup_inputs(seed: int = 0) -> dict:
    key = jax.random.key(seed)
    ks = jax.random.split(key, 32)

    def nrm(k, shape, scale):
        return jax.random.normal(k, shape, F32) * scale

    fw = HG_HEADS * HG_DK
    hvw = HG_HEADS * HG_DV
    hg_cols = 2 * fw + 2 * hvw
    hg_scale = jnp.ones((hg_cols,), F32).at[2 * fw:2 * fw + hvw].set(BETA) * D_MODEL ** -0.5
    qk = RET_HEADS * RET_DK
    rvw = RET_HEADS * RET_DV
    ret_cols = 2 * qk + 2 * rvw
    ret_scale = jnp.ones((ret_cols,), F32).at[2 * qk:2 * qk + rvw].set(BETA) * D_MODEL ** -0.5
    m_cols = M_DI + M_CONV_DIM + M_HEADS
    dt0 = jnp.exp(jax.random.uniform(ks[22], (N_SSM_LAYERS, M_HEADS), F32)
                  * (math.log(0.1) - math.log(0.001)) + math.log(0.001))
    return {
        'x_prompt': nrm(ks[0], (BATCH, SEQ, D_MODEL), 1.0),
        'x_sample': nrm(ks[1], (DEC_BATCH, DEC_SEQ, D_MODEL), 1.0),
        'state_hgrn': nrm(ks[2], (N_HG_LAYERS, DEC_BATCH, HG_HEADS, HG_DK, HG_DV), 0.5),
        'state_ret': nrm(ks[3], (N_RET_LAYERS, DEC_BATCH, RET_HEADS, RET_DK, RET_DV), 1.0),
        'state_ssm': nrm(ks[4], (N_SSM_LAYERS, DEC_BATCH, M_HEADS, M_DSTATE, M_HEADDIM), 0.5),
        'state_conv': nrm(ks[5], (N_SSM_LAYERS, DEC_BATCH, M_CONV - 1, M_CONV_DIM), 1.0),
        'meta_tokens': nrm(ks[6], (N_META, D_MODEL), 1.0),
        'ln_g': 1.0 + nrm(ks[7], (DEPTH, 3, D_MODEL), 0.02),
        'ln_b': nrm(ks[8], (DEPTH, 3, D_MODEL), 0.02),
        'ffn_w_gate': nrm(ks[9], (DEPTH, 2, D_MODEL, D_FF), D_MODEL ** -0.5),
        'ffn_w_up': nrm(ks[10], (DEPTH, 2, D_MODEL, D_FF), D_MODEL ** -0.5),
        'ffn_w_down': nrm(ks[11], (DEPTH, 2, D_FF, D_MODEL), BETA * D_FF ** -0.5),
        'hg_lb_logits': nrm(ks[12], (DEPTH, fw), 1.0),
        'hg_w_in': nrm(ks[13], (N_HG_LAYERS, D_MODEL, hg_cols), 1.0) * hg_scale,
        'hg_norm_g': 1.0 + nrm(ks[14], (N_HG_LAYERS, hvw), 0.02),
        'hg_w_o': nrm(ks[15], (N_HG_LAYERS, hvw, D_MODEL), BETA * hvw ** -0.5),
        'ret_w_in': nrm(ks[16], (N_RET_LAYERS, D_MODEL, ret_cols), 1.0) * ret_scale,
        'ret_norm_g': 1.0 + nrm(ks[17], (N_RET_LAYERS, rvw), 0.02),
        'ret_w_o': nrm(ks[18], (N_RET_LAYERS, rvw, D_MODEL), BETA * rvw ** -0.5),
        'm_w_in': nrm(ks[19], (N_SSM_LAYERS, D_MODEL, m_cols), D_MODEL ** -0.5),
        'm_conv_w': nrm(ks[20], (N_SSM_LAYERS, M_CONV, M_CONV_DIM), M_CONV ** -0.5),
        'm_conv_b': nrm(ks[21], (N_SSM_LAYERS, M_CONV_DIM), 0.02),
        'm_dt_bias': dt0 + jnp.log(-jnp.expm1(-dt0)),
        'm_a_log': jnp.log(jax.random.uniform(ks[23], (N_SSM_LAYERS, M_HEADS), F32, minval=1.0, maxval=16.0)),
        'm_d': 1.0 + nrm(ks[24], (N_SSM_LAYERS, M_HEADS), 0.1),
        'm_norm_g': 1.0 + nrm(ks[25], (N_SSM_LAYERS, M_DI), 0.02),
        'm_w_o': nrm(ks[26], (N_SSM_LAYERS, M_DI, D_MODEL), BETA * M_DI ** -0.5),
    }


def reference(x_prompt, x_sample, state_hgrn, state_ret, state_ssm, state_conv, meta_tokens, ln_g, ln_b,
              ffn_w_gate, ffn_w_up, ffn_w_down, hg_lb_logits, hg_w_in, hg_norm_g, hg_w_o,
              ret_w_in, ret_norm_g, ret_w_o, m_w_in, m_conv_w, m_conv_b, m_dt_bias, m_a_log, m_d,
              m_norm_g, m_w_o):
    p = {
        'ln_g': ln_g, 'ln_b': ln_b, 'ffn_w_gate': ffn_w_gate, 'ffn_w_up': ffn_w_up, 'ffn_w_down': ffn_w_down,
        'hg_lb_logits': hg_lb_logits, 'hg_w_in': hg_w_in, 'hg_norm_g': hg_norm_g, 'hg_w_o': hg_w_o,
        'ret_w_in': ret_w_in, 'ret_norm_g': ret_norm_g, 'ret_w_o': ret_w_o,
        'm_w_in': m_w_in, 'm_conv_w': m_conv_w, 'm_conv_b': m_conv_b, 'm_dt_bias': m_dt_bias,
        'm_a_log': m_a_log, 'm_d': m_d, 'm_norm_g': m_norm_g, 'm_w_o': m_w_o,
    }
    bp, sp, _ = x_prompt.shape
    h_p = jnp.concatenate(
        [jnp.broadcast_to(meta_tokens[None].astype(x_prompt.dtype), (bp, N_META, D_MODEL)), x_prompt], axis=1)
    pos_p = jnp.arange(N_META + sp)
    z_hg = jnp.zeros((N_HG_LAYERS, bp, HG_HEADS, HG_DK, HG_DV), F32)
    z_ret = jnp.zeros((N_RET_LAYERS, bp, RET_HEADS, RET_DK, RET_DV), F32)
    z_ssm = jnp.zeros((N_SSM_LAYERS, bp, M_HEADS, M_DSTATE, M_HEADDIM), F32)
    z_conv = jnp.zeros((N_SSM_LAYERS, bp, M_CONV - 1, M_CONV_DIM), x_prompt.dtype)
    h_p, hgrn_prompt, ret_prompt, ssm_prompt, conv_prompt = run_trunk(
        h_p, z_hg, z_ret, z_ssm, z_conv, pos_p, N_META, p)
    y_prompt = h_p[:, N_META:]
    pos_s = PAST_LEN + jnp.arange(x_sample.shape[1])
    y_sample, hgrn_sample, ret_sample, ssm_sample, conv_sample = run_trunk(
        x_sample, state_hgrn, state_ret, state_ssm, state_conv, pos_s, 0, p)
    return (y_prompt, y_sample, hgrn_prompt, hgrn_sample, ret_prompt, ret_sample,
            ssm_prompt, ssm_sample, conv_prompt, conv_sample)
```

```python
import functools
import math

import jax
import jax.numpy as jnp
from jax import lax
from jax.experimental import pallas as pl
from jax.experimental.pallas import tpu as pltpu

F32 = jnp.float32
BF16 = jnp.bfloat16

D_MODEL = 1024
DEPTH = 4
N_META = 16
PAST_LEN = 16384
CHUNK = 64
ALPHA = (2 * DEPTH) ** 0.25
LN_EPS = 1e-5
ROPE_BASE = 10000.0
HG_HEADS, HG_DK, HG_DV = 8, 128, 128
RET_HEADS, RET_DK, RET_DV = 4, 256, 512
M_DI, M_HEADDIM, M_HEADS, M_GROUPS, M_DSTATE, M_CONV = 2048, 64, 32, 8, 128, 4
M_CONV_DIM = M_DI + 2 * M_GROUPS * M_DSTATE
M_GROUP_W = M_DI // M_GROUPS
M_COLS = M_DI + M_CONV_DIM + M_HEADS
LANES = 128
M_COLS_PAD = ((M_COLS + LANES - 1) // LANES) * LANES
VMEM_LIMIT = 56 * 1024 * 1024


def _params(n_grid):
    return pltpu.CompilerParams(dimension_semantics=("arbitrary",) * n_grid, vmem_limit_bytes=VMEM_LIMIT)


def _resident(shape):
    return pl.BlockSpec(shape, lambda *_: (0,) * len(shape), pipeline_mode=pl.Buffered(1))


def _dot(a, b):
    return jnp.dot(a.astype(BF16), b.astype(BF16), preferred_element_type=F32)


def _dot_nt(a, b):
    return lax.dot_general(a.astype(BF16), b.astype(BF16), (((1,), (1,)), ((), ())), preferred_element_type=F32)


def _dot_tn(a, b):
    return lax.dot_general(a.astype(BF16), b.astype(BF16), (((0,), (0,)), ((), ())), preferred_element_type=F32)


def _split3(x):
    hi = x.astype(BF16)
    r = x - hi.astype(F32)
    mid = r.astype(BF16)
    lo = (r - mid.astype(F32)).astype(BF16)
    return hi, mid, lo


def _dot_exact_rhs(a01, x):
    a = a01.astype(BF16)
    hi, mid, lo = _split3(x)
    return (jnp.dot(a, hi, preferred_element_type=F32) + jnp.dot(a, mid, preferred_element_type=F32)
            + jnp.dot(a, lo, preferred_element_type=F32))


def _dot_exact_lhs(x, b01):
    b = b01.astype(BF16)
    hi, mid, lo = _split3(x)
    return (jnp.dot(hi, b, preferred_element_type=F32) + jnp.dot(mid, b, preferred_element_type=F32)
            + jnp.dot(lo, b, preferred_element_type=F32))


def _cumsum_rows(x):
    c = x.shape[0]
    tri = lax.broadcasted_iota(jnp.int32, (c, c), 0) >= lax.broadcasted_iota(jnp.int32, (c, c), 1)
    return _dot_exact_rhs(jnp.where(tri, 1.0, 0.0), x)


def _layer_norm(y, g, b):
    mu = jnp.mean(y, axis=-1, keepdims=True)
    d = y - mu
    var = jnp.mean(d * d, axis=-1, keepdims=True)
    return d * lax.rsqrt(var + LN_EPS) * g + b


def _ffn_ln_kernel(x_ref, wg_ref, wu_ref, wd_ref, g_ref, b_ref, o_ref, *, ff_chunk):
    x = x_ref[...]
    xb = x.astype(BF16)
    d_ff = wg_ref.shape[1]
    acc = jnp.zeros(x.shape, F32)
    for lo in range(0, d_ff, ff_chunk):
        gate = jnp.dot(xb, wg_ref[:, lo:lo + ff_chunk], preferred_element_type=F32)
        up = jnp.dot(xb, wu_ref[:, lo:lo + ff_chunk], preferred_element_type=F32)
        act = (jax.nn.silu(gate) * up).astype(BF16)
        acc = acc + jnp.dot(act, wd_ref[lo:lo + ff_chunk, :], preferred_element_type=F32)
    o_ref[...] = _layer_norm(ALPHA * x + 0.5 * acc, g_ref[...], b_ref[...])


def _ffn_ln(x, wg, wu, wd, g, b, tm):
    m, d = x.shape
    d_ff = wg.shape[1]
    return pl.pallas_call(
        functools.partial(_ffn_ln_kernel, ff_chunk=d_ff // 11),
        grid=(m // tm,),
        in_specs=[pl.BlockSpec((tm, d), lambda i: (i, 0)), _resident((d, d_ff)), _resident((d, d_ff)),
                  _resident((d_ff, d)), _resident((1, d)), _resident((1, d))],
        out_specs=pl.BlockSpec((tm, d), lambda i: (i, 0)),
        out_shape=jax.ShapeDtypeStruct((m, d), F32),
        compiler_params=_params(1), name="ffn_ln",
    )(x, wg, wu, wd, g.reshape(1, d), b.reshape(1, d))


def _proj_kernel(x_ref, w_ref, o_ref, *, col_chunk):
    xb = x_ref[...].astype(BF16)
    n = w_ref.shape[1]
    for lo in range(0, n, col_chunk):
        hi = min(lo + col_chunk, n)
        o_ref[:, lo:hi] = jnp.dot(xb, w_ref[:, lo:hi], preferred_element_type=F32)


def _proj(x, w, tm):
    m, d = x.shape
    n = w.shape[1]
    return pl.pallas_call(
        functools.partial(_proj_kernel, col_chunk=512),
        grid=(m // tm,),
        in_specs=[pl.BlockSpec((tm, d), lambda i: (i, 0)), _resident((d, n))],
        out_specs=pl.BlockSpec((tm, n), lambda i: (i, 0)),
        out_shape=jax.ShapeDtypeStruct((m, n), F32),
        compiler_params=_params(1), name="proj",
    )(x, w)


def _out_ln_kernel(h_ref, y_ref, w_ref, g_ref, b_ref, o_ref):
    m = jnp.dot(y_ref[...].astype(BF16), w_ref[...], preferred_element_type=F32)
    o_ref[...] = _layer_norm(ALPHA * h_ref[...] + m, g_ref[...], b_ref[...])


def _out_ln(h, y, w, g, b, tm):
    m, d = h.shape
    k = y.shape[1]
    return pl.pallas_call(
        _out_ln_kernel,
        grid=(m // tm,),
        in_specs=[pl.BlockSpec((tm, d), lambda i: (i, 0)), pl.BlockSpec((tm, k), lambda i: (i, 0)),
                  _resident((k, d)), _resident((1, d)), _resident((1, d))],
        out_specs=pl.BlockSpec((tm, d), lambda i: (i, 0)),
        out_shape=jax.ShapeDtypeStruct((m, d), F32),
        compiler_params=_params(1), name="out_ln",
    )(h, y, w, g.reshape(1, d), b.reshape(1, d))


def _hgrn_kernel(p_ref, lbl_ref, ng_ref, s0_ref, y_ref, so_ref, s_scr, *, c, n_chunks, layer):
    ci = pl.program_id(1)

    @pl.when(ci == 0)
    def _():
        s_scr[...] = s0_ref[0]

    fw = HG_HEADS * HG_DK
    logits = lbl_ref[...]
    e = jnp.exp(logits - jnp.max(logits, axis=0, keepdims=True))
    prob = e / jnp.sum(e, axis=0, keepdims=True)
    lb = prob[0:1] * 0.0
    run = prob[0:1]
    for r in range(1, layer + 1):
        run = run + prob[r:r + 1]
        lb = run - prob[0:1]

    z = p_ref[:, fw:2 * fw]
    f = lb + (1.0 - lb) * jax.nn.sigmoid(z)
    k_all = (1.0 - lb) * jax.nn.sigmoid(-z)
    g_all = _cumsum_rows(jnp.log(f))
    q_all = jax.nn.silu(p_ref[:, 0:fw])
    ones = jnp.ones((HG_DK, HG_DV), BF16)
    blk = 16 if c % 16 == 0 else 8
    row8 = lax.broadcasted_iota(jnp.int32, (8, HG_DK), 0)

    for h in range(HG_HEADS):
        sl = slice(h * HG_DK, (h + 1) * HG_DK)
        q, k, g = q_all[:, sl], k_all[:, sl], g_all[:, sl]
        v = p_ref[:, 2 * fw + h * HG_DV:2 * fw + (h + 1) * HG_DV]
        gate = p_ref[:, 2 * fw + HG_HEADS * HG_DV + h * HG_DV:2 * fw + HG_HEADS * HG_DV + (h + 1) * HG_DV]
        s = s_scr[h]
        o_inter = _dot(q * jnp.exp(g), s)
        o_rows = []
        for r0 in range(0, c, blk):
            pieces = []
            for t0 in range(r0, r0 + blk, 8):
                for u0 in range(r0, t0 + 8, 8):
                    qt, gt = q[t0:t0 + 8], g[t0:t0 + 8]
                    for j in range(u0, u0 + 8):
                        diff = gt - g[j:j + 1]
                        if u0 == t0:
                            diff = jnp.where(row8 >= (j - u0), diff, -jnp.inf)
                        pieces.append(qt * jnp.exp(diff) * k[j:j + 1])
            summed = jnp.dot(jnp.concatenate(pieces, axis=0).astype(BF16), ones, preferred_element_type=F32)
            idx = 0
            o_blk = []
            for t0 in range(r0, r0 + blk, 8):
                o_t = jnp.zeros((8, HG_DV), F32)
                for u0 in range(r0, t0 + 8, 8):
                    for j in range(u0, u0 + 8):
                        o_t = o_t + summed[idx * 8:(idx + 1) * 8] * v[j:j + 1]
                        idx += 1
                o_blk.append(o_t)
            o_blk = jnp.concatenate(o_blk, axis=0) if len(o_blk) > 1 else o_blk[0]
            if r0 > 0:
                g_ref0 = g[r0:r0 + 1]
                q_t = q[r0:r0 + blk] * jnp.exp(g[r0:r0 + blk] - g_ref0)
                k_t = k[0:r0] * jnp.exp(g_ref0 - g[0:r0])
                o_blk = o_blk + _dot(_dot_nt(q_t, k_t), v[0:r0])
            o_rows.append(o_blk)
        o = o_inter + (jnp.concatenate(o_rows, axis=0) if len(o_rows) > 1 else o_rows[0])

        g_last = g[c - 1:c]
        k_dec = k * jnp.exp(g_last - g)
        dec_col = jnp.exp(jnp.broadcast_to(g_last, (HG_DV, HG_DK)).T)
        s_scr[h] = dec_col * s + _dot_tn(k_dec, v)

        ms = jnp.mean(o * o, axis=-1, keepdims=True)
        y_ref[:, sl] = o * lax.rsqrt(ms + LN_EPS) * ng_ref[:, sl] * jax.nn.silu(gate)

    @pl.when(ci == n_chunks - 1)
    def _():
        so_ref[0] = s_scr[...]


def _hgrn_call(p, lb_logits, norm_g, s0, layer, *, row0, bsz, t, c, shared_state):
    n_chunks = t // c
    blk0 = row0 // c
    cols = p.shape[1]
    vw = HG_HEADS * HG_DV
    sshape = (1, HG_HEADS, HG_DK, HG_DV)
    s_idx = (lambda b, ci: (0, 0, 0, 0)) if shared_state else (lambda b, ci: (b, 0, 0, 0))
    return pl.pallas_call(
        functools.partial(_hgrn_kernel, c=c, n_chunks=n_chunks, layer=layer),
        grid=(bsz, n_chunks),
        in_specs=[pl.BlockSpec((c, cols), lambda b, ci: (blk0 + b * n_chunks + ci, 0)),
                  pl.BlockSpec(lb_logits.shape, lambda b, ci: (0, 0)),
                  pl.BlockSpec((1, vw), lambda b, ci: (0, 0)),
                  pl.BlockSpec(sshape, s_idx)],
        out_specs=[pl.BlockSpec((c, vw), lambda b, ci: (b * n_chunks + ci, 0)),
                   pl.BlockSpec(sshape, lambda b, ci: (b, 0, 0, 0))],
        out_shape=[jax.ShapeDtypeStruct((bsz * t, vw), F32),
                   jax.ShapeDtypeStruct((bsz,) + sshape[1:], F32)],
        scratch_shapes=[pltpu.VMEM(sshape[1:], F32)],
        compiler_params=_params(2), name="hgrn",
    )(p, lb_logits, norm_g.reshape(1, vw), s0)


def _ret_kernel(p_ref, cos_ref, sin_ref, ng_ref, s0_ref, y_ref, so_ref, s_scr, *, c, n_chunks):
    ci = pl.program_id(1)

    @pl.when(ci == 0)
    def _():
        s_scr[...] = s0_ref[0]

    qk = RET_HEADS * RET_DK
    vw = RET_HEADS * RET_DV
    half = RET_DK // 2
    cos, sin = cos_ref[...], sin_ref[...]
    row = lax.broadcasted_iota(jnp.int32, (c, c), 0)
    col = lax.broadcasted_iota(jnp.int32, (c, c), 1)
    lag = (row - col).astype(F32)
    pos = lax.broadcasted_iota(jnp.int32, (c, 1), 0).astype(F32)

    def rot(x):
        x1, x2 = x[:, :half], x[:, half:]
        return jnp.concatenate([x1 * cos - x2 * sin, x1 * sin + x2 * cos], axis=-1)

    for h in range(RET_HEADS):
        log_gamma = math.log(1.0 - 2.0 ** (-5.0 - h))
        q = rot(p_ref[:, h * RET_DK:(h + 1) * RET_DK])
        k = rot(p_ref[:, qk + h * RET_DK:qk + (h + 1) * RET_DK]) * (RET_DK ** -0.5)
        v = p_ref[:, 2 * qk + h * RET_DV:2 * qk + (h + 1) * RET_DV]
        gate = p_ref[:, 2 * qk + vw + h * RET_DV:2 * qk + vw + (h + 1) * RET_DV]
        s = s_scr[h]
        dec = jnp.exp(jnp.where(row >= col, lag * log_gamma, -jnp.inf))
        att = _dot_nt(q, k) * dec
        o = _dot(att, v) + jnp.exp((pos + 1.0) * log_gamma) * _dot(q, s)
        k_dec = k * jnp.exp((c - 1.0 - pos) * log_gamma)
        s_scr[h] = math.exp(c * log_gamma) * s + _dot_tn(k_dec, v)

        mu = jnp.mean(o, axis=-1, keepdims=True)
        d = o - mu
        var = jnp.mean(d * d, axis=-1, keepdims=True)
        sl = slice(h * RET_DV, (h + 1) * RET_DV)
        y_ref[:, sl] = d * lax.rsqrt(var + LN_EPS) * ng_ref[:, sl] * jax.nn.silu(gate)

    @pl.when(ci == n_chunks - 1)
    def _():
        so_ref[0] = s_scr[...]


def _ret_call(p, cos, sin, norm_g, s0, *, row0, bsz, t, c, shared_state):
    n_chunks = t // c
    blk0 = row0 // c
    cols = p.shape[1]
    vw = RET_HEADS * RET_DV
    sshape = (1, RET_HEADS, RET_DK, RET_DV)
    s_idx = (lambda b, ci: (0, 0, 0, 0)) if shared_state else (lambda b, ci: (b, 0, 0, 0))
    return pl.pallas_call(
        functools.partial(_ret_kernel, c=c, n_chunks=n_chunks),
        grid=(bsz, n_chunks),
        in_specs=[pl.BlockSpec((c, cols), lambda b, ci: (blk0 + b * n_chunks + ci, 0)),
                  pl.BlockSpec((c, RET_DK // 2), lambda b, ci: (ci, 0)),
                  pl.BlockSpec((c, RET_DK // 2), lambda b, ci: (ci, 0)),
                  pl.BlockSpec((1, vw), lambda b, ci: (0, 0)),
                  pl.BlockSpec(sshape, s_idx)],
        out_specs=[pl.BlockSpec((c, vw), lambda b, ci: (b * n_chunks + ci, 0)),
                   pl.BlockSpec(sshape, lambda b, ci: (b, 0, 0, 0))],
        out_shape=[jax.ShapeDtypeStruct((bsz * t, vw), F32),
                   jax.ShapeDtypeStruct((bsz,) + sshape[1:], F32)],
        scratch_shapes=[pltpu.VMEM(sshape[1:], F32)],
        compiler_params=_params(2), name="retention",
    )(p, cos, sin, norm_g.reshape(1, vw), s0)


def _mamba_kernel(p_ref, cw_ref, cb_ref, dtb_ref, alog_ref, dskip_ref, ng_ref, s0_ref, c0_ref,
                  y_ref, so_ref, co_ref, s_scr, x_scr, *, c, n_chunks):
    ci = pl.program_id(1)
    hpg = M_HEADS // M_GROUPS
    keep = M_CONV - 1

    @pl.when(ci == 0)
    def _():
        for grp in range(M_GROUPS):
            s_scr[grp] = jnp.concatenate([s0_ref[0, grp * hpg + i] for i in range(hpg)], axis=-1)
        x_scr[8 - keep:8, :] = c0_ref[0]

    x_scr[8:8 + c, :] = p_ref[:, M_DI:M_DI + M_CONV_DIM]
    conv = cb_ref[...] + x_scr[8 - keep:8 - keep + c, :] * cw_ref[0:1, :]
    for w in range(1, M_CONV):
        conv = conv + x_scr[8 - keep + w:8 - keep + w + c, :] * cw_ref[w:w + 1, :]
    tail = x_scr[8 + c - keep:8 + c, :]
    x_scr[8 - keep:8, :] = tail

    @pl.when(ci == n_chunks - 1)
    def _():
        co_ref[0] = tail

    xbc = jax.nn.silu(conv)
    gn = M_GROUPS * M_DSTATE

    dt = jax.nn.softplus(p_ref[:, M_DI + M_CONV_DIM:M_DI + M_CONV_DIM + LANES] + dtb_ref[...])
    g = _cumsum_rows(dt * -jnp.exp(alog_ref[...]))
    g_t = g.T
    g_last = g[c - 1:c]
    expand = (lax.broadcasted_iota(jnp.int32, (LANES, M_DI), 1) // M_HEADDIM
              == lax.broadcasted_iota(jnp.int32, (LANES, M_DI), 0))
    expand = jnp.where(expand, 1.0, 0.0)
    dt_x = _dot_exact_lhs(dt, expand)
    eg_x = _dot_exact_lhs(jnp.exp(g), expand)
    egl_x = eg_x[c - 1:c]
    k_fac = jnp.exp(g_last - g)
    tri = lax.broadcasted_iota(jnp.int32, (c, c), 0) >= lax.broadcasted_iota(jnp.int32, (c, c), 1)
    lane_head = lax.broadcasted_iota(jnp.int32, (c, M_GROUP_W), 1) // M_HEADDIM

    for grp in range(M_GROUPS):
        gsl = slice(grp * M_GROUP_W, (grp + 1) * M_GROUP_W)
        xs = xbc[:, gsl]
        bmat = xbc[:, M_DI + grp * M_DSTATE:M_DI + (grp + 1) * M_DSTATE]
        cmat = xbc[:, M_DI + gn + grp * M_DSTATE:M_DI + gn + (grp + 1) * M_DSTATE]
        xdt = xs * dt_x[:, gsl]
        s = s_scr[grp]
        att = _dot_nt(cmat, bmat)
        o = eg_x[:, gsl] * _dot(cmat, s)
        upd = jnp.zeros((M_DSTATE, M_GROUP_W), F32)
        for i in range(hpg):
            hd = grp * hpg + i
            g_col = jnp.broadcast_to(g[:, hd:hd + 1], (c, c))
            g_row = jnp.broadcast_to(g_t[hd:hd + 1, :], (c, c))
            dec = jnp.exp(jnp.where(tri, g_col - g_row, -jnp.inf))
            xdt_h = jnp.where(lane_head == i, xdt, 0.0)
            o = o + _dot(att * dec, xdt_h)
            upd = upd + _dot_tn(bmat * jnp.broadcast_to(k_fac[:, hd:hd + 1], (c, M_DSTATE)), xdt_h)
        s_scr[grp] = egl_x[:, gsl] * s + upd

        y = (o + dskip_ref[:, gsl] * xs) * jax.nn.silu(p_ref[:, gsl])
        ms = jnp.mean(y * y, axis=-1, keepdims=True)
        y_ref[:, gsl] = y * lax.rsqrt(ms + LN_EPS) * ng_ref[:, gsl]

    @pl.when(ci == n_chunks - 1)
    def _():
        for grp in range(M_GROUPS):
            s = s_scr[grp]
            for i in range(hpg):
                so_ref[0, grp * hpg + i] = s[:, i * M_HEADDIM:(i + 1) * M_HEADDIM]


def _mamba_call(p, conv_w, conv_b, dt_bias, a_log, d_skip, norm_g, s0, conv0, *, row0, bsz, t, c, shared_state):
    n_chunks = t // c
    blk0 = row0 // c
    cols = p.shape[1]
    sshape = (1, M_HEADS, M_DSTATE, M_HEADDIM)
    cshape = (1, M_CONV - 1, M_CONV_DIM)
    s_idx = (lambda b, ci: (0, 0, 0, 0)) if shared_state else (lambda b, ci: (b, 0, 0, 0))
    c_idx = (lambda b, ci: (0, 0, 0)) if shared_state else (lambda b, ci: (b, 0, 0))
    pad = LANES - M_HEADS
    const = lambda shape: pl.BlockSpec(shape, lambda b, ci: (0,) * len(shape))
    return pl.pallas_call(
        functools.partial(_mamba_kernel, c=c, n_chunks=n_chunks),
        grid=(bsz, n_chunks),
        in_specs=[pl.BlockSpec((c, cols), lambda b, ci: (blk0 + b * n_chunks + ci, 0)),
                  const((M_CONV, M_CONV_DIM)), const((1, M_CONV_DIM)), const((1, LANES)), const((1, LANES)),
                  const((1, M_DI)), const((1, M_DI)),
                  pl.BlockSpec(sshape, s_idx), pl.BlockSpec(cshape, c_idx)],
        out_specs=[pl.BlockSpec((c, M_DI), lambda b, ci: (b * n_chunks + ci, 0)),
                   pl.BlockSpec(sshape, lambda b, ci: (b, 0, 0, 0)),
                   pl.BlockSpec(cshape, lambda b, ci: (b, 0, 0))],
        out_shape=[jax.ShapeDtypeStruct((bsz * t, M_DI), F32),
                   jax.ShapeDtypeStruct((bsz,) + sshape[1:], F32),
                   jax.ShapeDtypeStruct((bsz,) + cshape[1:], F32)],
        scratch_shapes=[pltpu.VMEM((M_GROUPS, M_DSTATE, M_GROUP_W), F32), pltpu.VMEM((8 + c, M_CONV_DIM), F32)],
        compiler_params=_params(2), name="mamba",
    )(p, conv_w, conv_b.reshape(1, M_CONV_DIM), jnp.pad(dt_bias, (0, pad)).reshape(1, LANES),
      jnp.pad(a_log, (0, pad)).reshape(1, LANES), jnp.repeat(d_skip, M_HEADDIM).reshape(1, M_DI),
      norm_g.reshape(1, M_DI), s0, conv0)


def _rope_tables(pos):
    half = RET_DK // 2
    inv_freq = ROPE_BASE ** (-jnp.arange(half, dtype=F32) / half)
    ang = pos.astype(F32)[:, None] * inv_freq[None, :]
    return jnp.cos(ang), jnp.sin(ang)


def kernel(x_prompt, x_sample, state_hgrn, state_ret, state_ssm, state_conv, meta_tokens, ln_g, ln_b,
           ffn_w_gate, ffn_w_up, ffn_w_down, hg_lb_logits, hg_w_in, hg_norm_g, hg_w_o,
           ret_w_in, ret_norm_g, ret_w_o, m_w_in, m_conv_w, m_conv_b, m_dt_bias, m_a_log, m_d,
           m_norm_g, m_w_o):
    bp, sp, d = x_prompt.shape
    bs, ss, _ = x_sample.shape
    n_p, n_s = bp * sp, bs * ss
    n_rows = n_p + n_s + N_META
    tm = 512 if n_rows >= 4096 else 64
    m_tot = ((n_rows + tm - 1) // tm) * tm
    c_p = math.gcd(sp, CHUNK)
    c_s = math.gcd(ss, CHUNK)
    groups = {
        "meta": dict(row0=n_p + n_s, bsz=1, t=N_META, c=math.gcd(N_META, CHUNK), shared_state=True),
        "prompt": dict(row0=0, bsz=bp, t=sp, c=c_p, shared_state=True),
        "sample": dict(row0=n_p, bsz=bs, t=ss, c=c_s, shared_state=False),
    }
    h = jnp.concatenate([x_prompt.reshape(n_p, d), x_sample.reshape(n_s, d), meta_tokens.astype(F32),
                         jnp.zeros((m_tot - n_rows, d), F32)], axis=0)

    rope = {"meta": _rope_tables(jnp.arange(N_META)), "prompt": _rope_tables(N_META + jnp.arange(sp)),
            "sample": _rope_tables(PAST_LEN + jnp.arange(ss))}

    def assemble(y_meta, y_prompt, y_sample):
        return jnp.concatenate([y_prompt, y_sample, y_meta, jnp.zeros((m_tot - n_rows, y_meta.shape[1]), F32)], axis=0)

    new = {k: [] for k in ("hg_p", "hg_s", "ret_p", "ret_s", "ssm_p", "ssm_s", "conv_p", "conv_s")}
    for i in range(DEPTH):
        h = _ffn_ln(h, ffn_w_gate[i, 0].astype(BF16), ffn_w_up[i, 0].astype(BF16), ffn_w_down[i, 0].astype(BF16),
                    ln_g[i, 0], ln_b[i, 0], tm)
        kind, j = i % 3, i // 3
        if kind == 0:
            p = _proj(h, hg_w_in[j].astype(BF16), tm)
            zero = jnp.zeros((1, HG_HEADS, HG_DK, HG_DV), F32)
            run = lambda name, s0: _hgrn_call(p, hg_lb_logits, hg_norm_g[j], s0, i, **groups[name])
            y_m, s_m = run("meta", zero)
            y_p, s_p = run("prompt", s_m)
            y_s, s_s = run("sample", state_hgrn[j])
            new["hg_p"].append(s_p)
            new["hg_s"].append(s_s)
            w_o = hg_w_o[j]
        elif kind == 1:
            p = _proj(h, ret_w_in[j].astype(BF16), tm)
            zero = jnp.zeros((1, RET_HEADS, RET_DK, RET_DV), F32)
            run = lambda name, s0: _ret_call(p, rope[name][0], rope[name][1], ret_norm_g[j], s0, **groups[name])
            y_m, s_m = run("meta", zero)
            y_p, s_p = run("prompt", s_m)
            y_s, s_s = run("sample", state_ret[j])
            new["ret_p"].append(s_p)
            new["ret_s"].append(s_s)
            w_o = ret_w_o[j]
        else:
            w_in = jnp.pad(m_w_in[j], ((0, 0), (0, M_COLS_PAD - M_COLS))).astype(BF16)
            p = _proj(h, w_in, tm)
            zero = jnp.zeros((1, M_HEADS, M_DSTATE, M_HEADDIM), F32)
            zero_c = jnp.zeros((1, M_CONV - 1, M_CONV_DIM), F32)
            run = lambda name, s0, c0: _mamba_call(p, m_conv_w[j], m_conv_b[j], m_dt_bias[j], m_a_log[j], m_d[j],
                                                   m_norm_g[j], s0, c0, **groups[name])
            y_m, s_m, c_m = run("meta", zero, zero_c)
            y_p, s_p, c_pp = run("prompt", s_m, c_m)
            y_s, s_s, c_ss = run("sample", state_ssm[j], state_conv[j])
            new["ssm_p"].append(s_p)
            new["ssm_s"].append(s_s)
            new["conv_p"].append(c_pp)
            new["conv_s"].append(c_ss)
            w_o = m_w_o[j]
        h = _out_ln(h, assemble(y_m, y_p, y_s), w_o.astype(BF16), ln_g[i, 1], ln_b[i, 1], tm)
        h = _ffn_ln(h, ffn_w_gate[i, 1].astype(BF16), ffn_w_up[i, 1].astype(BF16), ffn_w_down[i, 1].astype(BF16),
                    ln_g[i, 2], ln_b[i, 2], tm)

    y_prompt = h[:n_p].reshape(bp, sp, d)
    y_sample = h[n_p:n_p + n_s].reshape(bs, ss, d)
    st = lambda k: jnp.stack(new[k])
    return (y_prompt, y_sample, st("hg_p"), st("hg_s"), st("ret_p"), st("ret_s"),
            st("ssm_p"), st("ssm_s"), st("conv_p"), st("conv_s"))
```

```python
import functools
import math

import jax
import jax.numpy as jnp
from jax import lax
from jax.experimental import pallas as pl
from jax.experimental.pallas import tpu as pltpu

F32 = jnp.float32
BF16 = jnp.bfloat16

D_MODEL = 1024
DEPTH = 4
N_META = 16
PAST_LEN = 16384
CHUNK = 64
ALPHA = (2 * DEPTH) ** 0.25
LN_EPS = 1e-5
ROPE_BASE = 10000.0
HG_HEADS, HG_DK, HG_DV = 8, 128, 128
RET_HEADS, RET_DK, RET_DV = 4, 256, 512
M_DI, M_HEADDIM, M_HEADS, M_GROUPS, M_DSTATE, M_CONV = 2048, 64, 32, 8, 128, 4
M_CONV_DIM = M_DI + 2 * M_GROUPS * M_DSTATE
M_GROUP_W = M_DI // M_GROUPS
M_COLS = M_DI + M_CONV_DIM + M_HEADS
LANES = 128
BF16_ROWS = 16
M_COLS_PAD = ((M_COLS + LANES - 1) // LANES) * LANES
VMEM_LIMIT = 56 * 1024 * 1024
MAX_ROW_TILE = 528


def _params(n_grid):
    return pltpu.CompilerParams(dimension_semantics=("arbitrary",) * n_grid, vmem_limit_bytes=VMEM_LIMIT)


def _resident(shape):
    return pl.BlockSpec(shape, lambda *_: (0,) * len(shape), pipeline_mode=pl.Buffered(1))


def _dot(a, b):
    return jnp.dot(a.astype(BF16), b.astype(BF16), preferred_element_type=F32)


def _dot_nt(a, b):
    return lax.dot_general(a.astype(BF16), b.astype(BF16), (((1,), (1,)), ((), ())), preferred_element_type=F32)


def _dot_tn(a, b):
    return lax.dot_general(a.astype(BF16), b.astype(BF16), (((0,), (0,)), ((), ())), preferred_element_type=F32)


def _split3(x):
    hi = x.astype(BF16)
    r = x - hi.astype(F32)
    mid = r.astype(BF16)
    lo = (r - mid.astype(F32)).astype(BF16)
    return hi, mid, lo


def _dot_exact_rhs(a01, x):
    a = a01.astype(BF16)
    hi, mid, lo = _split3(x)
    return (jnp.dot(a, hi, preferred_element_type=F32) + jnp.dot(a, mid, preferred_element_type=F32)
            + jnp.dot(a, lo, preferred_element_type=F32))


def _dot_exact_lhs(x, b01):
    b = b01.astype(BF16)
    hi, mid, lo = _split3(x)
    return (jnp.dot(hi, b, preferred_element_type=F32) + jnp.dot(mid, b, preferred_element_type=F32)
            + jnp.dot(lo, b, preferred_element_type=F32))


def _cumsum_rows(x, c):
    n = x.shape[0]
    row = lax.broadcasted_iota(jnp.int32, (n, n), 0)
    col = lax.broadcasted_iota(jnp.int32, (n, n), 1)
    tri = (row >= col) if n == c else ((row >= col) & (row // c == col // c))
    return _dot_exact_rhs(jnp.where(tri, 1.0, 0.0), x)


def _layer_norm(y, g, b):
    mu = jnp.mean(y, axis=-1, keepdims=True)
    d = y - mu
    var = jnp.mean(d * d, axis=-1, keepdims=True)
    return d * lax.rsqrt(var + LN_EPS) * g + b


def _ffn_ln_kernel(*refs, n_src, n_dst, tiles, ff_chunk):
    srcs, (wg_ref, wu_ref, wd_ref, g_ref, b_ref) = refs[:n_src], refs[n_src:n_src + 5]
    dsts = refs[n_src + 5:n_src + 5 + n_dst]
    i = pl.program_id(0)
    if n_src == 1:
        x = srcs[0][...]
    else:
        x_scr = refs[-1]
        n_meta = srcs[2].shape[0]

        @pl.when(i < tiles[0])
        def _():
            x_scr[...] = srcs[0][...]

        @pl.when((i >= tiles[0]) & (i < tiles[0] + tiles[1]))
        def _():
            x_scr[...] = srcs[1][...]

        @pl.when(i == tiles[0] + tiles[1])
        def _():
            x_scr[0:n_meta, :] = srcs[2][...]
            if x_scr.shape[0] > n_meta:
                x_scr[n_meta:, :] = jnp.zeros((x_scr.shape[0] - n_meta, x_scr.shape[1]), F32)

        x = x_scr[...]
    xb = x.astype(BF16)
    d_ff = wg_ref.shape[1]
    acc = jnp.zeros(x.shape, F32)
    for lo in range(0, d_ff, ff_chunk):
        gate = jnp.dot(xb, wg_ref[:, lo:lo + ff_chunk], preferred_element_type=F32)
        up = jnp.dot(xb, wu_ref[:, lo:lo + ff_chunk], preferred_element_type=F32)
        act = (jax.nn.silu(gate) * up).astype(BF16)
        acc = acc + jnp.dot(act, wd_ref[lo:lo + ff_chunk, :], preferred_element_type=F32)
    res = _layer_norm(ALPHA * x + 0.5 * acc, g_ref[...], b_ref[...])
    if n_dst == 1:
        dsts[0][...] = res
    else:
        @pl.when(i < tiles[0])
        def _():
            dsts[0][...] = res

        @pl.when(i >= tiles[0])
        def _():
            dsts[1][...] = res


def _ffn_weights(wg, wu, wd, g, b):
    d, d_ff = wg.shape
    specs = [_resident((d, d_ff)), _resident((d, d_ff)), _resident((d_ff, d)), _resident((1, d)), _resident((1, d))]
    return specs, (wg, wu, wd, g.reshape(1, d), b.reshape(1, d))


def _ffn_ln(x, wg, wu, wd, g, b, tm, n_rows_out=None):
    m, d = x.shape
    m_out = m if n_rows_out is None else n_rows_out
    w_specs, w_args = _ffn_weights(wg, wu, wd, g, b)
    return pl.pallas_call(
        functools.partial(_ffn_ln_kernel, n_src=1, n_dst=1, tiles=None, ff_chunk=wg.shape[1] // 11),
        grid=(m_out // tm,),
        in_specs=[pl.BlockSpec((tm, d), lambda i: (i, 0))] + w_specs,
        out_specs=pl.BlockSpec((tm, d), lambda i: (i, 0)),
        out_shape=jax.ShapeDtypeStruct((m_out, d), F32),
        compiler_params=_params(1), name="ffn_ln",
    )(x, *w_args)


def _ffn_ln_first(x_p, x_s, x_m, wg, wu, wd, g, b, tm):
    (n_p, d), n_s, n_m = x_p.shape, x_s.shape[0], x_m.shape[0]
    tp, ts = n_p // tm, n_s // tm
    w_specs, w_args = _ffn_weights(wg, wu, wd, g, b)
    return pl.pallas_call(
        functools.partial(_ffn_ln_kernel, n_src=3, n_dst=1, tiles=(tp, ts), ff_chunk=wg.shape[1] // 11),
        grid=(tp + ts + 1,),
        in_specs=[pl.BlockSpec((tm, d), lambda i: (jnp.minimum(i, tp - 1), 0)),
                  pl.BlockSpec((tm, d), lambda i: (jnp.clip(i - tp, 0, ts - 1), 0)),
                  pl.BlockSpec((n_m, d), lambda i: (0, 0))] + w_specs,
        out_specs=pl.BlockSpec((tm, d), lambda i: (i, 0)),
        out_shape=jax.ShapeDtypeStruct((n_p + n_s + n_m, d), F32),
        scratch_shapes=[pltpu.VMEM((tm, d), F32)],
        compiler_params=_params(1), name="ffn_ln_first",
    )(x_p, x_s, x_m, *w_args)


def _ffn_ln_last(x, n_p, n_s, wg, wu, wd, g, b, tm):
    d = x.shape[1]
    tp, ts = n_p // tm, n_s // tm
    w_specs, w_args = _ffn_weights(wg, wu, wd, g, b)
    return pl.pallas_call(
        functools.partial(_ffn_ln_kernel, n_src=1, n_dst=2, tiles=(tp, ts), ff_chunk=wg.shape[1] // 11),
        grid=(tp + ts,),
        in_specs=[pl.BlockSpec((tm, d), lambda i: (i, 0))] + w_specs,
        out_specs=[pl.BlockSpec((tm, d), lambda i: (jnp.minimum(i, tp - 1), 0)),
                   pl.BlockSpec((tm, d), lambda i: (jnp.clip(i - tp, 0, ts - 1), 0))],
        out_shape=[jax.ShapeDtypeStruct((n_p, d), F32), jax.ShapeDtypeStruct((n_s, d), F32)],
        compiler_params=_params(1), name="ffn_ln_last",
    )(x, *w_args)


def _proj_kernel(x_ref, w_ref, o_ref, *, col_chunk):
    xb = x_ref[...].astype(BF16)
    n = w_ref.shape[1]
    for lo in range(0, n, col_chunk):
        hi = min(lo + col_chunk, n)
        o_ref[:, lo:hi] = jnp.dot(xb, w_ref[:, lo:hi], preferred_element_type=F32)


def _proj(x, w, tm):
    m, d = x.shape
    n = w.shape[1]
    return pl.pallas_call(
        functools.partial(_proj_kernel, col_chunk=512),
        grid=(m // tm,),
        in_specs=[pl.BlockSpec((tm, d), lambda i: (i, 0)), _resident((d, n))],
        out_specs=pl.BlockSpec((tm, n), lambda i: (i, 0)),
        out_shape=jax.ShapeDtypeStruct((m, n), F32),
        compiler_params=_params(1), name="proj",
    )(x, w)


def _out_ln_kernel(h_ref, y_ref, w_ref, g_ref, b_ref, o_ref):
    m = jnp.dot(y_ref[...], w_ref[...], preferred_element_type=F32)
    o_ref[...] = _layer_norm(ALPHA * h_ref[...] + m, g_ref[...], b_ref[...])


def _out_ln(h, y, w, g, b, tm, n_rows_out=None):
    m, d = h.shape
    m_out = m if n_rows_out is None else n_rows_out
    k = y.shape[1]
    return pl.pallas_call(
        _out_ln_kernel,
        grid=(m_out // tm,),
        in_specs=[pl.BlockSpec((tm, d), lambda i: (i, 0)), pl.BlockSpec((tm, k), lambda i: (i, 0)),
                  _resident((k, d)), _resident((1, d)), _resident((1, d))],
        out_specs=pl.BlockSpec((tm, d), lambda i: (i, 0)),
        out_shape=jax.ShapeDtypeStruct((m_out, d), F32),
        compiler_params=_params(1), name="out_ln",
    )(h, y, w, g.reshape(1, d), b.reshape(1, d))


def _mixer_call(kernel_fn, name, p, consts, states0, out_bufs, *, y_width, state_shapes, scratch,
                row0, bsz, t, c, nb, shared_state, slot):
    n_chunks = t // c
    assert nb == 1 or n_chunks == 1
    rows = nb * c
    assert row0 % rows == 0 and bsz % nb == 0
    blk0 = row0 // rows
    cols = p.shape[1]
    row_idx = lambda sb, ci: (blk0 + sb * n_chunks + ci, 0)
    in_specs = [pl.BlockSpec((rows, cols), row_idx)]
    in_specs += [pl.BlockSpec(a.shape, lambda sb, ci, _n=a.ndim: (0,) * _n) for a in consts]
    for s0, shp in zip(states0, state_shapes):
        zeros = (0,) * len(shp)
        in_specs.append(pl.BlockSpec((nb,) + shp, (lambda sb, ci, _z=zeros: (0,) + _z) if shared_state
                                     else (lambda sb, ci, _z=zeros: (sb,) + _z)))
    out_specs = [pl.BlockSpec((rows, y_width), row_idx)]
    for shp in state_shapes:
        zeros = (0,) * len(shp)
        out_specs.append(pl.BlockSpec((1, nb) + shp, lambda sb, ci, _z=zeros: (slot, sb) + _z))
    n_in = 1 + len(consts) + len(states0)
    aliases = {}
    args = [p, *consts, *states0]
    if out_bufs is not None:
        for k, buf in enumerate(out_bufs):
            if buf is not None:
                aliases[len(args)] = k
                in_specs.append(pl.BlockSpec(memory_space=pl.ANY))
                args.append(buf)
    return n_in, in_specs, out_specs, aliases, args, (bsz // nb, n_chunks)


def _run_mixer(kernel_fn, name, p, consts, states0, out_shapes, out_bufs, scratch, **kw):
    n_in, in_specs, out_specs, aliases, args, grid = _mixer_call(kernel_fn, name, p, consts, states0, out_bufs, scratch=scratch, **kw)
    n_alias = len(args) - n_in
    return pl.pallas_call(
        functools.partial(kernel_fn, n_alias=n_alias),
        grid=grid, in_specs=in_specs, out_specs=out_specs, out_shape=out_shapes,
        scratch_shapes=scratch, input_output_aliases=aliases,
        compiler_params=_params(2), name=name,
    )(*args)


def _hgrn_kernel(p_ref, lbl_ref, ng_ref, s0_ref, *rest, c, nb, n_chunks, layer, n_alias):
    y_ref, so_ref, s_scr = rest[n_alias:]
    ci = pl.program_id(1)

    @pl.when(ci == 0)
    def _():
        s_scr[...] = s0_ref[...]

    fw = HG_HEADS * HG_DK
    logits = lbl_ref[...]
    e = jnp.exp(logits - jnp.max(logits, axis=0, keepdims=True))
    prob = e / jnp.sum(e, axis=0, keepdims=True)
    lb = prob[0:1] * 0.0
    run = prob[0:1]
    for r in range(1, layer + 1):
        run = run + prob[r:r + 1]
        lb = run - prob[0:1]

    z = p_ref[:, fw:2 * fw]
    f = lb + (1.0 - lb) * jax.nn.sigmoid(z)
    k_all = (1.0 - lb) * jax.nn.sigmoid(-z)
    g_all = _cumsum_rows(jnp.log(f), c)
    q_all = jax.nn.silu(p_ref[:, 0:fw])
    ones = jnp.ones((HG_DK, HG_DV), BF16)
    blk = 16 if c % 16 == 0 else 8
    row8 = lax.broadcasted_iota(jnp.int32, (8, HG_DK), 0)

    for h in range(HG_HEADS):
        sl = slice(h * HG_DK, (h + 1) * HG_DK)
        y_rows = []
        for sq in range(nb):
            ro = sq * c
            q, k, g = q_all[ro:ro + c, sl], k_all[ro:ro + c, sl], g_all[ro:ro + c, sl]
            v = p_ref[ro:ro + c, 2 * fw + h * HG_DV:2 * fw + (h + 1) * HG_DV]
            gate = p_ref[ro:ro + c, 2 * fw + HG_HEADS * HG_DV + h * HG_DV:2 * fw + HG_HEADS * HG_DV + (h + 1) * HG_DV]
            s = s_scr[sq, h]
            o_inter = _dot(q * jnp.exp(g), s)
            o_rows = []
            for r0 in range(0, c, blk):
                pieces = []
                for t0 in range(r0, r0 + blk, 8):
                    for u0 in range(r0, t0 + 8, 8):
                        qt, gt = q[t0:t0 + 8], g[t0:t0 + 8]
                        for j in range(u0, u0 + 8):
                            diff = gt - g[j:j + 1]
                            if u0 == t0:
                                diff = jnp.where(row8 >= (j - u0), diff, -jnp.inf)
                            pieces.append(qt * jnp.exp(diff) * k[j:j + 1])
                summed = jnp.dot(jnp.concatenate(pieces, axis=0).astype(BF16), ones, preferred_element_type=F32)
                idx = 0
                o_blk = []
                for t0 in range(r0, r0 + blk, 8):
                    o_t = jnp.zeros((8, HG_DV), F32)
                    for u0 in range(r0, t0 + 8, 8):
                        for j in range(u0, u0 + 8):
                            o_t = o_t + summed[idx * 8:(idx + 1) * 8] * v[j:j + 1]
                            idx += 1
                    o_blk.append(o_t)
                o_blk = jnp.concatenate(o_blk, axis=0) if len(o_blk) > 1 else o_blk[0]
                if r0 > 0:
                    g_ref0 = g[r0:r0 + 1]
                    q_t = q[r0:r0 + blk] * jnp.exp(g[r0:r0 + blk] - g_ref0)
                    k_t = k[0:r0] * jnp.exp(g_ref0 - g[0:r0])
                    o_blk = o_blk + _dot(_dot_nt(q_t, k_t), v[0:r0])
                o_rows.append(o_blk)
            o = o_inter + (jnp.concatenate(o_rows, axis=0) if len(o_rows) > 1 else o_rows[0])

            g_last = g[c - 1:c]
            k_dec = k * jnp.exp(g_last - g)
            dec_col = jnp.exp(jnp.broadcast_to(g_last, (HG_DV, HG_DK)).T)
            s_scr[sq, h] = dec_col * s + _dot_tn(k_dec, v)

            ms = jnp.mean(o * o, axis=-1, keepdims=True)
            y_rows.append(o * lax.rsqrt(ms + LN_EPS) * ng_ref[:, sl] * jax.nn.silu(gate))
        y_ref[:, sl] = (jnp.concatenate(y_rows, axis=0) if nb > 1 else y_rows[0]).astype(y_ref.dtype)

    @pl.when(ci == n_chunks - 1)
    def _():
        so_ref[0] = s_scr[...]


def _ret_kernel(p_ref, cos_ref, sin_ref, ng_ref, s0_ref, *rest, c, nb, n_chunks, n_alias):
    y_ref, so_ref, s_scr = rest[n_alias:]
    ci = pl.program_id(1)

    @pl.when(ci == 0)
    def _():
        s_scr[...] = s0_ref[...]

    qk = RET_HEADS * RET_DK
    vw = RET_HEADS * RET_DV
    half = RET_DK // 2
    row = lax.broadcasted_iota(jnp.int32, (c, c), 0)
    col = lax.broadcasted_iota(jnp.int32, (c, c), 1)
    lag = (row - col).astype(F32)
    pos = lax.broadcasted_iota(jnp.int32, (c, 1), 0).astype(F32)
    cos = cos_ref[pl.ds(pl.multiple_of(ci * c, c), c), :]
    sin = sin_ref[pl.ds(pl.multiple_of(ci * c, c), c), :]

    def rot(x):
        x1, x2 = x[:, :half], x[:, half:]
        return jnp.concatenate([x1 * cos - x2 * sin, x1 * sin + x2 * cos], axis=-1)

    for h in range(RET_HEADS):
        log_gamma = math.log(1.0 - 2.0 ** (-5.0 - h))
        sl = slice(h * RET_DV, (h + 1) * RET_DV)
        y_rows = []
        for sq in range(nb):
            ro = sq * c
            q = rot(p_ref[ro:ro + c, h * RET_DK:(h + 1) * RET_DK])
            k = rot(p_ref[ro:ro + c, qk + h * RET_DK:qk + (h + 1) * RET_DK]) * (RET_DK ** -0.5)
            v = p_ref[ro:ro + c, 2 * qk + h * RET_DV:2 * qk + (h + 1) * RET_DV]
            gate = p_ref[ro:ro + c, 2 * qk + vw + h * RET_DV:2 * qk + vw + (h + 1) * RET_DV]
            s = s_scr[sq, h]
            dec = jnp.exp(jnp.where(row >= col, lag * log_gamma, -jnp.inf))
            att = _dot_nt(q, k) * dec
            o = _dot(att, v) + jnp.exp((pos + 1.0) * log_gamma) * _dot(q, s)
            k_dec = k * jnp.exp((c - 1.0 - pos) * log_gamma)
            s_scr[sq, h] = math.exp(c * log_gamma) * s + _dot_tn(k_dec, v)

            mu = jnp.mean(o, axis=-1, keepdims=True)
            d = o - mu
            var = jnp.mean(d * d, axis=-1, keepdims=True)
            y_rows.append(d * lax.rsqrt(var + LN_EPS) * ng_ref[:, sl] * jax.nn.silu(gate))
        y_ref[:, sl] = (jnp.concatenate(y_rows, axis=0) if nb > 1 else y_rows[0]).astype(y_ref.dtype)

    @pl.when(ci == n_chunks - 1)
    def _():
        so_ref[0] = s_scr[...]


def _mamba_kernel(p_ref, cw_ref, cb_ref, dtb_ref, alog_ref, dskip_ref, ng_ref, s0_ref, c0_ref, *rest,
                  c, nb, n_chunks, n_alias):
    y_ref, so_ref, co_ref, s_scr, x_scr = rest[n_alias:]
    ci = pl.program_id(1)
    hpg = M_HEADS // M_GROUPS
    keep = M_CONV - 1
    gn = M_GROUPS * M_DSTATE

    @pl.when(ci == 0)
    def _():
        for sq in range(nb):
            for grp in range(M_GROUPS):
                s_scr[sq, grp] = jnp.concatenate([s0_ref[sq, grp * hpg + i] for i in range(hpg)], axis=-1)
            x_scr[sq, 8 - keep:8, :] = c0_ref[sq]

    dt = jax.nn.softplus(p_ref[:, M_DI + M_CONV_DIM:M_DI + M_CONV_DIM + LANES] + dtb_ref[...])
    g_all = _cumsum_rows(dt * -jnp.exp(alog_ref[...]), c)
    expand = (lax.broadcasted_iota(jnp.int32, (LANES, M_DI), 1) // M_HEADDIM
              == lax.broadcasted_iota(jnp.int32, (LANES, M_DI), 0))
    expand = jnp.where(expand, 1.0, 0.0)
    dt_x_all = _dot_exact_lhs(dt, expand)
    eg_x_all = _dot_exact_lhs(jnp.exp(g_all), expand)
    tri = lax.broadcasted_iota(jnp.int32, (c, c), 0) >= lax.broadcasted_iota(jnp.int32, (c, c), 1)
    lane_head = lax.broadcasted_iota(jnp.int32, (c, M_GROUP_W), 1) // M_HEADDIM
    y_rows = [[] for _ in range(M_GROUPS)]

    for sq in range(nb):
        ro = sq * c
        x_scr[sq, 8:8 + c, :] = p_ref[ro:ro + c, M_DI:M_DI + M_CONV_DIM]
        conv = cb_ref[...] + x_scr[sq, 8 - keep:8 - keep + c, :] * cw_ref[0:1, :]
        for w in range(1, M_CONV):
            conv = conv + x_scr[sq, 8 - keep + w:8 - keep + w + c, :] * cw_ref[w:w + 1, :]
        tail = x_scr[sq, 8 + c - keep:8 + c, :]
        x_scr[sq, 8 - keep:8, :] = tail

        @pl.when(ci == n_chunks - 1)
        def _():
            co_ref[0, sq] = tail

        xbc = jax.nn.silu(conv)
        g = g_all[ro:ro + c]
        g_t = g.T
        g_last = g[c - 1:c]
        dt_x, eg_x = dt_x_all[ro:ro + c], eg_x_all[ro:ro + c]
        egl_x = eg_x[c - 1:c]
        k_fac = jnp.exp(g_last - g)

        for grp in range(M_GROUPS):
            gsl = slice(grp * M_GROUP_W, (grp + 1) * M_GROUP_W)
            xs = xbc[:, gsl]
            bmat = xbc[:, M_DI + grp * M_DSTATE:M_DI + (grp + 1) * M_DSTATE]
            cmat = xbc[:, M_DI + gn + grp * M_DSTATE:M_DI + gn + (grp + 1) * M_DSTATE]
            xdt = xs * dt_x[:, gsl]
            s = s_scr[sq, grp]
            att = _dot_nt(cmat, bmat)
            o = eg_x[:, gsl] * _dot(cmat, s)
            upd = jnp.zeros((M_DSTATE, M_GROUP_W), F32)
            for i in range(hpg):
                hd = grp * hpg + i
                g_col = jnp.broadcast_to(g[:, hd:hd + 1], (c, c))
                g_row = jnp.broadcast_to(g_t[hd:hd + 1, :], (c, c))
                dec = jnp.exp(jnp.where(tri, g_col - g_row, -jnp.inf))
                xdt_h = jnp.where(lane_head == i, xdt, 0.0)
                o = o + _dot(att * dec, xdt_h)
                upd = upd + _dot_tn(bmat * jnp.broadcast_to(k_fac[:, hd:hd + 1], (c, M_DSTATE)), xdt_h)
            s_scr[sq, grp] = egl_x[:, gsl] * s + upd

            y = (o + dskip_ref[:, gsl] * xs) * jax.nn.silu(p_ref[ro:ro + c, gsl])
            ms = jnp.mean(y * y, axis=-1, keepdims=True)
            y_rows[grp].append(y * lax.rsqrt(ms + LN_EPS) * ng_ref[:, gsl])

    for grp in range(M_GROUPS):
        gsl = slice(grp * M_GROUP_W, (grp + 1) * M_GROUP_W)
        y_ref[:, gsl] = (jnp.concatenate(y_rows[grp], axis=0) if nb > 1 else y_rows[grp][0]).astype(y_ref.dtype)

    @pl.when(ci == n_chunks - 1)
    def _():
        for sq in range(nb):
            for grp in range(M_GROUPS):
                s = s_scr[sq, grp]
                for i in range(hpg):
                    so_ref[0, sq, grp * hpg + i] = s[:, i * M_HEADDIM:(i + 1) * M_HEADDIM]


def _rope_tables(pos):
    half = RET_DK // 2
    inv_freq = ROPE_BASE ** (-jnp.arange(half, dtype=F32) / half)
    ang = pos.astype(F32)[:, None] * inv_freq[None, :]
    return jnp.cos(ang), jnp.sin(ang)


def _row_tile(n_rows):
    for cand in range(min(MAX_ROW_TILE, n_rows) // BF16_ROWS * BF16_ROWS, 0, -BF16_ROWS):
        if n_rows % cand == 0:
            return cand
    raise ValueError(f"no row tile for {n_rows} rows")


def kernel(x_prompt, x_sample, state_hgrn, state_ret, state_ssm, state_conv, meta_tokens, ln_g, ln_b,
           ffn_w_gate, ffn_w_up, ffn_w_down, hg_lb_logits, hg_w_in, hg_norm_g, hg_w_o,
           ret_w_in, ret_norm_g, ret_w_o, m_w_in, m_conv_w, m_conv_b, m_dt_bias, m_a_log, m_d,
           m_norm_g, m_w_o):
    bp, sp, d = x_prompt.shape
    bs, ss, _ = x_sample.shape
    n_p, n_s = bp * sp, bs * ss
    n_rows = n_p + n_s + N_META
    tm = _row_tile(n_rows)
    tm_edge = math.gcd(math.gcd(n_p, n_s), 512)
    c_s = math.gcd(ss, CHUNK)
    nb_s = BF16_ROWS // c_s if (c_s < BF16_ROWS and ss == c_s) else 1
    groups = {
        "meta": dict(row0=n_p + n_s, bsz=1, t=N_META, c=math.gcd(N_META, CHUNK), nb=1, shared_state=True),
        "prompt": dict(row0=0, bsz=bp, t=sp, c=math.gcd(sp, CHUNK), nb=1, shared_state=True),
        "sample": dict(row0=n_p, bsz=bs, t=ss, c=c_s, nb=nb_s, shared_state=False),
    }
    bsz_of = {"prompt": bp, "sample": bs}
    rope = {"meta": _rope_tables(jnp.arange(N_META)), "prompt": _rope_tables(N_META + jnp.arange(sp)),
            "sample": _rope_tables(PAST_LEN + jnp.arange(ss))}
    bf = lambda w: w.astype(BF16)

    def run_mixer(kernel_fn, name, p, consts_of, states_of, y_width, state_shapes, n_slots, slot, scratch_of, bufs):
        y = None
        meta_states = None
        new_bufs = {}
        for grp in ("meta", "prompt", "sample"):
            gk = groups[grp]
            states0 = meta_states if grp == "prompt" else states_of(grp)
            if grp == "meta":
                st_shapes = [jax.ShapeDtypeStruct((1, 1) + shp, F32) for shp in state_shapes]
                st_bufs = [None] * len(state_shapes)
                g_slot = 0
            else:
                st_shapes = [jax.ShapeDtypeStruct((n_slots, bsz_of[grp]) + shp, F32) for shp in state_shapes]
                st_bufs = bufs[grp] if bufs is not None else [None] * len(state_shapes)
                g_slot = slot
            outs = _run_mixer(
                functools.partial(kernel_fn, c=gk["c"], nb=gk["nb"], n_chunks=gk["t"] // gk["c"]), name + "_" + grp,
                p, consts_of(grp), states0, [jax.ShapeDtypeStruct((n_rows, y_width), BF16)] + st_shapes,
                [y] + st_bufs, scratch_of(gk), y_width=y_width, state_shapes=state_shapes, slot=g_slot, **gk)
            y = outs[0]
            if grp == "meta":
                meta_states = [o[0] for o in outs[1:]]
            else:
                new_bufs[grp] = list(outs[1:])
        return y, new_bufs

    hg_bufs = ret_bufs = ssm_bufs = None
    n_hg, n_ret, n_ssm = state_hgrn.shape[0], state_ret.shape[0], state_ssm.shape[0]
    h = None
    y_prompt = y_sample = None
    for i in range(DEPTH):
        ffn1 = (bf(ffn_w_gate[i, 0]), bf(ffn_w_up[i, 0]), bf(ffn_w_down[i, 0]), ln_g[i, 0], ln_b[i, 0])
        if i == 0:
            h = _ffn_ln_first(x_prompt.reshape(n_p, d), x_sample.reshape(n_s, d), meta_tokens.astype(F32), *ffn1, tm_edge)
        else:
            h = _ffn_ln(h, *ffn1, tm)
        kind, j = i % 3, i // 3
        if kind == 0:
            p = _proj(h, bf(hg_w_in[j]), tm)
            shp = (HG_HEADS, HG_DK, HG_DV)
            zero = [jnp.zeros((1,) + shp, F32)]
            y, hg_bufs = run_mixer(
                functools.partial(_hgrn_kernel, layer=i), "hgrn", p,
                lambda grp: [hg_lb_logits, hg_norm_g[j].reshape(1, -1)],
                lambda grp: zero if grp == "meta" else [state_hgrn[j]],
                HG_HEADS * HG_DV, [shp], n_hg, j,
                lambda gk: [pltpu.VMEM((gk["nb"],) + shp, F32)], hg_bufs)
            w_o = hg_w_o[j]
        elif kind == 1:
            p = _proj(h, bf(ret_w_in[j]), tm)
            shp = (RET_HEADS, RET_DK, RET_DV)
            zero = [jnp.zeros((1,) + shp, F32)]
            y, ret_bufs = run_mixer(
                _ret_kernel, "retention", p,
                lambda grp: [rope[grp][0], rope[grp][1], ret_norm_g[j].reshape(1, -1)],
                lambda grp: zero if grp == "meta" else [state_ret[j]],
                RET_HEADS * RET_DV, [shp], n_ret, j,
                lambda gk: [pltpu.VMEM((gk["nb"],) + shp, F32)], ret_bufs)
            w_o = ret_w_o[j]
        else:
            p = _proj(h, bf(jnp.pad(m_w_in[j], ((0, 0), (0, M_COLS_PAD - M_COLS)))), tm)
            shp, cshp = (M_HEADS, M_DSTATE, M_HEADDIM), (M_CONV - 1, M_CONV_DIM)
            zero = [jnp.zeros((1,) + shp, F32), jnp.zeros((1,) + cshp, F32)]
            pad = LANES - M_HEADS
            consts = [m_conv_w[j], m_conv_b[j].reshape(1, -1), jnp.pad(m_dt_bias[j], (0, pad)).reshape(1, LANES),
                      jnp.pad(m_a_log[j], (0, pad)).reshape(1, LANES), jnp.repeat(m_d[j], M_HEADDIM).reshape(1, -1),
                      m_norm_g[j].reshape(1, -1)]
            y, ssm_bufs = run_mixer(
                _mamba_kernel, "mamba", p, lambda grp: consts,
                lambda grp: zero if grp == "meta" else [state_ssm[j], state_conv[j]],
                M_DI, [shp, cshp], n_ssm, j,
                lambda gk: [pltpu.VMEM((gk["nb"], M_GROUPS, M_DSTATE, M_GROUP_W), F32),
                            pltpu.VMEM((gk["nb"], 8 + gk["c"], M_CONV_DIM), F32)], ssm_bufs)
            w_o = m_w_o[j]
        ffn2 = (bf(ffn_w_gate[i, 1]), bf(ffn_w_up[i, 1]), bf(ffn_w_down[i, 1]), ln_g[i, 2], ln_b[i, 2])
        if i < DEPTH - 1:
            h = _out_ln(h, y, bf(w_o), ln_g[i, 1], ln_b[i, 1], tm)
            h = _ffn_ln(h, *ffn2, tm)
        else:
            h = _out_ln(h, y, bf(w_o), ln_g[i, 1], ln_b[i, 1], tm_edge, n_rows_out=n_p + n_s)
            y_prompt, y_sample = _ffn_ln_last(h, n_p, n_s, *ffn2, tm_edge)

    return (y_prompt.reshape(bp, sp, d), y_sample.reshape(bs, ss, d),
            hg_bufs["prompt"][0], hg_bufs["sample"][0], ret_bufs["prompt"][0], ret_bufs["sample"][0],
            ssm_bufs["prompt"][0], ssm_bufs["sample"][0], ssm_bufs["prompt"][1], ssm_bufs["sample"][1])
```

```python
import functools
import math

import jax
import jax.numpy as jnp
from jax import lax
from jax.experimental import pallas as pl
from jax.experimental.pallas import tpu as pltpu

F32 = jnp.float32
BF16 = jnp.bfloat16

D_MODEL = 1024
DEPTH = 4
N_META = 16
PAST_LEN = 16384
CHUNK = 64
ALPHA = (2 * DEPTH) ** 0.25
LN_EPS = 1e-5
ROPE_BASE = 10000.0
HG_HEADS, HG_DK, HG_DV = 8, 128, 128
RET_HEADS, RET_DK, RET_DV = 4, 256, 512
RET_CHUNK = 128
M_DI, M_HEADDIM, M_HEADS, M_GROUPS, M_DSTATE, M_CONV = 2048, 64, 32, 8, 128, 4
M_CONV_DIM = M_DI + 2 * M_GROUPS * M_DSTATE
M_GROUP_W = M_DI // M_GROUPS
M_HPG = M_HEADS // M_GROUPS
M_COLS = M_DI + M_CONV_DIM + M_HEADS
LANES = 128
BF16_ROWS = 16
M_COLS_PAD = ((M_COLS + LANES - 1) // LANES) * LANES
CONV_COLS = 512
VMEM_LIMIT = 56 * 1024 * 1024
MAX_ROW_TILE = 528
LOG2E = 1.4426950408889634


def _params(n_grid, flags=None):
    return pltpu.CompilerParams(dimension_semantics=("arbitrary",) * n_grid, vmem_limit_bytes=VMEM_LIMIT, flags=flags)


def _resident(shape):
    return pl.BlockSpec(shape, lambda *_: (0,) * len(shape), pipeline_mode=pl.Buffered(1))


def _resident_at(shape, lead):
    return pl.BlockSpec((None,) * len(lead) + tuple(shape), lambda *_: tuple(lead) + (0,) * len(shape),
                        pipeline_mode=pl.Buffered(1))


def _dot(a, b):
    return jnp.dot(a.astype(BF16), b.astype(BF16), preferred_element_type=F32)


def _dot_nt(a, b):
    return lax.dot_general(a.astype(BF16), b.astype(BF16), (((1,), (1,)), ((), ())), preferred_element_type=F32)


def _dot_tn(a, b):
    return lax.dot_general(a.astype(BF16), b.astype(BF16), (((0,), (0,)), ((), ())), preferred_element_type=F32)


def _split(x, n_parts):
    parts = []
    r = x
    for _ in range(n_parts - 1):
        hi = r.astype(BF16).astype(F32)
        parts.append(hi)
        r = r - hi
    parts.append(r)
    return parts


def _dot_exact_rhs(a01, x, n_parts=3):
    n = x.shape[1]
    out = jnp.dot(a01, jnp.concatenate(_split(x, n_parts), axis=1).astype(BF16), preferred_element_type=F32)
    res = out[:, :n]
    for i in range(1, n_parts):
        res = res + out[:, i * n:(i + 1) * n]
    return res


def _dot_exact_lhs(xs, b01, n_parts=3):
    r = xs[0].shape[0]
    parts = [p for x in xs for p in _split(x, n_parts)]
    out = jnp.dot(jnp.concatenate(parts, axis=0).astype(BF16), b01, preferred_element_type=F32)
    res = []
    for i in range(len(xs)):
        acc = out[n_parts * i * r:(n_parts * i + 1) * r]
        for k in range(1, n_parts):
            acc = acc + out[(n_parts * i + k) * r:(n_parts * i + k + 1) * r]
        res.append(acc)
    return res


def _cumsum_rows(x, c):
    n = x.shape[0]
    row = lax.broadcasted_iota(jnp.int32, (n, n), 0)
    col = lax.broadcasted_iota(jnp.int32, (n, n), 1)
    tri = (row >= col) if n == c else ((row >= col) & (row // c == col // c))
    return _dot_exact_rhs(jnp.where(tri, 1.0, 0.0).astype(BF16), x)


def _layer_norm(y, g, b):
    mu = jnp.mean(y, axis=-1, keepdims=True)
    d = y - mu
    var = jnp.mean(d * d, axis=-1, keepdims=True)
    return d * lax.rsqrt(var + LN_EPS) * g + b


def _ffn_ln_kernel(*refs, n_src, n_dst, tiles, ff_chunk):
    srcs, (wg_ref, wu_ref, wd_ref, g_ref, b_ref) = refs[:n_src], refs[n_src:n_src + 5]
    dsts = refs[n_src + 5:n_src + 5 + n_dst]
    i = pl.program_id(0)
    if n_src == 1:
        x = srcs[0][...]
    else:
        x_scr = refs[-1]
        n_meta = srcs[2].shape[0]

        @pl.when(i < tiles[0])
        def _():
            x_scr[...] = srcs[0][...]

        @pl.when((i >= tiles[0]) & (i < tiles[0] + tiles[1]))
        def _():
            x_scr[...] = srcs[1][...]

        @pl.when(i == tiles[0] + tiles[1])
        def _():
            x_scr[0:n_meta, :] = srcs[2][...]
            if x_scr.shape[0] > n_meta:
                x_scr[n_meta:, :] = jnp.zeros((x_scr.shape[0] - n_meta, x_scr.shape[1]), F32)

        x = x_scr[...]
    xb = x.astype(BF16)
    d_ff = wg_ref.shape[1]
    acc = jnp.zeros(x.shape, F32)
    for lo in range(0, d_ff, ff_chunk):
        gate = jnp.dot(xb, wg_ref[:, lo:lo + ff_chunk], preferred_element_type=F32)
        up = jnp.dot(xb, wu_ref[:, lo:lo + ff_chunk], preferred_element_type=F32)
        act = (jax.nn.silu(gate) * up).astype(BF16)
        acc = acc + jnp.dot(act, wd_ref[lo:lo + ff_chunk, :], preferred_element_type=F32)
    res = _layer_norm(ALPHA * x + 0.5 * acc, g_ref[...], b_ref[...])
    if n_dst == 1:
        dsts[0][...] = res
    else:
        @pl.when(i < tiles[0])
        def _():
            dsts[0][...] = res

        @pl.when(i >= tiles[0])
        def _():
            dsts[1][...] = res


def _ffn_weights(ffn, layer, which):
    wg, wu, wd, ln_g, ln_b = ffn
    d, d_ff = wg.shape[2:]
    ln_row = 0 if which == 0 else 2
    specs = [_resident_at((d, d_ff), (layer, which)), _resident_at((d, d_ff), (layer, which)),
             _resident_at((d_ff, d), (layer, which)), _resident_at((1, d), (layer, ln_row)),
             _resident_at((1, d), (layer, ln_row))]
    return specs, (wg, wu, wd, ln_g, ln_b), d_ff // 11


def _ffn_ln(x, ffn, layer, which, tm):
    m, d = x.shape
    w_specs, w_args, ff_chunk = _ffn_weights(ffn, layer, which)
    return pl.pallas_call(
        functools.partial(_ffn_ln_kernel, n_src=1, n_dst=1, tiles=None, ff_chunk=ff_chunk),
        grid=(m // tm,),
        in_specs=[pl.BlockSpec((tm, d), lambda i: (i, 0))] + w_specs,
        out_specs=pl.BlockSpec((tm, d), lambda i: (i, 0)),
        out_shape=jax.ShapeDtypeStruct((m, d), F32),
        compiler_params=_params(1), name="ffn_ln",
    )(x, *w_args)


def _ffn_ln_first(x_p, x_s, x_m, ffn, layer, which, tm):
    (n_p, d), n_s, n_m = x_p.shape, x_s.shape[0], x_m.shape[0]
    tp, ts = n_p // tm, n_s // tm
    w_specs, w_args, ff_chunk = _ffn_weights(ffn, layer, which)
    return pl.pallas_call(
        functools.partial(_ffn_ln_kernel, n_src=3, n_dst=1, tiles=(tp, ts), ff_chunk=ff_chunk),
        grid=(tp + ts + 1,),
        in_specs=[pl.BlockSpec((tm, d), lambda i: (jnp.minimum(i, tp - 1), 0)),
                  pl.BlockSpec((tm, d), lambda i: (jnp.clip(i - tp, 0, ts - 1), 0)),
                  pl.BlockSpec((n_m, d), lambda i: (0, 0))] + w_specs,
        out_specs=pl.BlockSpec((tm, d), lambda i: (i, 0)),
        out_shape=jax.ShapeDtypeStruct((n_p + n_s + n_m, d), F32),
        scratch_shapes=[pltpu.VMEM((tm, d), F32)],
        compiler_params=_params(1), name="ffn_ln_first",
    )(x_p, x_s, x_m, *w_args)


def _ffn_ln_last(x, n_p, n_s, ffn, layer, which, tm):
    d = x.shape[1]
    tp, ts = n_p // tm, n_s // tm
    w_specs, w_args, ff_chunk = _ffn_weights(ffn, layer, which)
    return pl.pallas_call(
        functools.partial(_ffn_ln_kernel, n_src=1, n_dst=2, tiles=(tp, ts), ff_chunk=ff_chunk),
        grid=(tp + ts,),
        in_specs=[pl.BlockSpec((tm, d), lambda i: (i, 0))] + w_specs,
        out_specs=[pl.BlockSpec((tm, d), lambda i: (jnp.minimum(i, tp - 1), 0)),
                   pl.BlockSpec((tm, d), lambda i: (jnp.clip(i - tp, 0, ts - 1), 0))],
        out_shape=[jax.ShapeDtypeStruct((n_p, d), F32), jax.ShapeDtypeStruct((n_s, d), F32)],
        compiler_params=_params(1), name="ffn_ln_last",
    )(x, *w_args)


def _proj_kernel(x_ref, w_ref, o_ref, *, col_chunk):
    xb = x_ref[...].astype(BF16)
    n = w_ref.shape[1]
    for lo in range(0, n, col_chunk):
        hi = min(lo + col_chunk, n)
        o_ref[:, lo:hi] = jnp.dot(xb, w_ref[:, lo:hi], preferred_element_type=F32)


def _proj(x, w, layer, tm):
    m, d = x.shape
    n = w.shape[2]
    return pl.pallas_call(
        functools.partial(_proj_kernel, col_chunk=512),
        grid=(m // tm,),
        in_specs=[pl.BlockSpec((tm, d), lambda i: (i, 0)), _resident_at((d, n), (layer,))],
        out_specs=pl.BlockSpec((tm, n), lambda i: (i, 0)),
        out_shape=jax.ShapeDtypeStruct((m, n), F32),
        compiler_params=_params(1), name="proj",
    )(x, w)


def _out_ln_kernel(h_ref, y_ref, w_ref, g_ref, b_ref, o_ref):
    m = jnp.dot(y_ref[...], w_ref[...], preferred_element_type=F32)
    o_ref[...] = _layer_norm(ALPHA * h_ref[...] + m, g_ref[...], b_ref[...])


def _out_ln(h, y, w, w_layer, ln_g, ln_b, layer, tm, n_rows_out=None):
    m, d = h.shape
    m_out = m if n_rows_out is None else n_rows_out
    k = y.shape[1]
    return pl.pallas_call(
        _out_ln_kernel,
        grid=(m_out // tm,),
        in_specs=[pl.BlockSpec((tm, d), lambda i: (i, 0)), pl.BlockSpec((tm, k), lambda i: (i, 0)),
                  _resident_at((k, d), (w_layer,)), _resident_at((1, d), (layer, 1)), _resident_at((1, d), (layer, 1))],
        out_specs=pl.BlockSpec((tm, d), lambda i: (i, 0)),
        out_shape=jax.ShapeDtypeStruct((m_out, d), F32),
        compiler_params=_params(1), name="out_ln",
    )(h, y, w, ln_g, ln_b)


def _run_mixer(kernel_fn, name, p, consts, states0, state_layer, out_shapes, out_bufs, scratch, *, y_width,
               state_shapes, row0, bsz, t, c, nb, shared_state, slot):
    n_chunks = t // c
    assert nb == 1 or n_chunks == 1
    rows = nb * c
    assert row0 % rows == 0 and bsz % nb == 0
    blk0 = row0 // rows
    row_idx = lambda sb, ci: (blk0 + sb * n_chunks + ci, 0)
    in_specs = [pl.BlockSpec((rows, p.shape[1]), row_idx)]
    in_specs += [pl.BlockSpec(a.shape, lambda sb, ci, _n=a.ndim: (0,) * _n) for a in consts]
    for shp in state_shapes:
        zeros = (0,) * len(shp)
        in_specs.append(pl.BlockSpec((None, nb) + shp, (lambda sb, ci, _z=zeros: (state_layer, 0) + _z) if shared_state
                                     else (lambda sb, ci, _z=zeros: (state_layer, sb) + _z)))
    out_specs = [pl.BlockSpec((rows, y_width), row_idx)]
    for shp in state_shapes:
        zeros = (0,) * len(shp)
        out_specs.append(pl.BlockSpec((None, nb) + shp, lambda sb, ci, _z=zeros: (slot, sb) + _z))
    args = [p, *consts, *states0]
    n_in = len(args)
    aliases = {}
    for k, buf in enumerate(out_bufs):
        if buf is not None:
            aliases[len(args)] = k
            in_specs.append(pl.BlockSpec(memory_space=pl.ANY))
            args.append(buf)
    return pl.pallas_call(
        functools.partial(kernel_fn, n_alias=len(args) - n_in),
        grid=(bsz // nb, n_chunks), in_specs=in_specs, out_specs=out_specs, out_shape=out_shapes,
        scratch_shapes=scratch, input_output_aliases=aliases,
        compiler_params=_params(2), name=name,
    )(*args)


def _hgrn_piece_plan(blk):
    return [(t0, u0, t0 == u0) for t0 in range(0, blk, 8) for u0 in range(0, t0 + 8, 8)]


def _hgrn_kernel(p_ref, lbl_ref, ng_ref, s0_ref, *rest, c, nb, n_chunks, layer, n_alias):
    y_ref, so_ref, s_scr = rest[n_alias:]
    ci = pl.program_id(1)

    @pl.when(ci == 0)
    def _():
        s_scr[...] = s0_ref[...]

    fw = HG_HEADS * HG_DK
    logits = lbl_ref[...]
    e = jnp.exp(logits - jnp.max(logits, axis=0, keepdims=True))
    prob = e / jnp.sum(e, axis=0, keepdims=True)
    lb = prob[0:1] * 0.0
    run = prob[0:1]
    for r in range(1, layer + 1):
        run = run + prob[r:r + 1]
        lb = run - prob[0:1]

    z = p_ref[:, fw:2 * fw]
    ez = jnp.exp(-jnp.abs(z))
    s_big = 1.0 / (1.0 + ez)
    s_small = ez * s_big
    pos = z >= 0.0
    f = lb + (1.0 - lb) * jnp.where(pos, s_big, s_small)
    k_all = (1.0 - lb) * jnp.where(pos, s_small, s_big)
    g_all = _cumsum_rows(jnp.log(f), c)
    g2_all = g_all * LOG2E
    q_all = jax.nn.silu(p_ref[:, 0:fw])
    blk = 16 if c % 16 == 0 else 8
    plan = _hgrn_piece_plan(blk)
    row8 = lax.broadcasted_iota(jnp.int32, (8, HG_DK), 0)
    ones = jnp.ones((HG_DK, HG_DV), BF16)

    for h in range(HG_HEADS):
        sl = slice(h * HG_DK, (h + 1) * HG_DK)
        y_rows = []
        for sq in range(nb):
            ro = sq * c
            q, k, g2 = q_all[ro:ro + c, sl], k_all[ro:ro + c, sl], g2_all[ro:ro + c, sl]
            v = p_ref[ro:ro + c, 2 * fw + h * HG_DV:2 * fw + (h + 1) * HG_DV]
            gate = p_ref[ro:ro + c, 2 * fw + HG_HEADS * HG_DV + h * HG_DV:2 * fw + HG_HEADS * HG_DV + (h + 1) * HG_DV]
            s = s_scr[sq, h]
            o_inter = _dot(q * jnp.exp2(g2), s)
            o_rows = []
            pieces = []
            for r0 in range(0, c, blk):
                for (t0, u0, masked) in plan:
                    qt, gt = q[r0 + t0:r0 + t0 + 8], g2[r0 + t0:r0 + t0 + 8]
                    for j in range(r0 + u0, r0 + u0 + 8):
                        diff = gt - g2[j:j + 1]
                        if masked:
                            diff = jnp.where(row8 >= (j - r0 - u0), diff, -jnp.inf)
                        pieces.append(qt * jnp.exp2(diff) * k[j:j + 1])
            summed = jnp.dot(jnp.concatenate(pieces, axis=0).astype(BF16), ones, preferred_element_type=F32)
            idx = 0
            for r0 in range(0, c, blk):
                o_tiles = {}
                for (t0, u0, _) in plan:
                    for j in range(r0 + u0, r0 + u0 + 8):
                        term = summed[idx * 8:(idx + 1) * 8] * v[j:j + 1]
                        o_tiles[t0] = term if t0 not in o_tiles else o_tiles[t0] + term
                        idx += 1
                o_blk = jnp.concatenate([o_tiles[t0] for t0 in range(0, blk, 8)], axis=0) if blk > 8 else o_tiles[0]
                if r0 > 0:
                    g_ref0 = g2[r0:r0 + 1]
                    q_t = q[r0:r0 + blk] * jnp.exp2(g2[r0:r0 + blk] - g_ref0)
                    k_t = k[0:r0] * jnp.exp2(g_ref0 - g2[0:r0])
                    o_blk = o_blk + _dot(_dot_nt(q_t, k_t), v[0:r0])
                o_rows.append(o_blk)
            o = o_inter + (jnp.concatenate(o_rows, axis=0) if len(o_rows) > 1 else o_rows[0])

            g_last = g2[c - 1:c]
            k_dec = k * jnp.exp2(g_last - g2)
            dec_col = jnp.exp2(jnp.broadcast_to(g_last, (HG_DV, HG_DK)).T)
            s_scr[sq, h] = dec_col * s + _dot_tn(k_dec, v)

            ms = jnp.mean(o * o, axis=-1, keepdims=True)
            y_rows.append(o * lax.rsqrt(ms + LN_EPS) * ng_ref[:, sl] * jax.nn.silu(gate))
        y_ref[:, sl] = (jnp.concatenate(y_rows, axis=0) if nb > 1 else y_rows[0]).astype(y_ref.dtype)

    @pl.when(ci == n_chunks - 1)
    def _():
        so_ref[...] = s_scr[...]


def _ret_kernel(p_ref, cos_ref, sin_ref, ng_ref, s0_ref, *rest, c, nb, n_chunks, n_alias):
    y_ref, so_ref, s_scr = rest[n_alias:]
    ci = pl.program_id(1)

    @pl.when(ci == 0)
    def _():
        s_scr[...] = s0_ref[...]

    qk = RET_HEADS * RET_DK
    vw = RET_HEADS * RET_DV
    half = RET_DK // 2
    row = lax.broadcasted_iota(jnp.int32, (c, c), 0)
    col = lax.broadcasted_iota(jnp.int32, (c, c), 1)
    lag = (row - col).astype(F32)
    pos = lax.broadcasted_iota(jnp.int32, (c, 1), 0).astype(F32)
    cos = cos_ref[pl.ds(pl.multiple_of(ci * c, c), c), :]
    sin = sin_ref[pl.ds(pl.multiple_of(ci * c, c), c), :]

    def rot(x):
        x1, x2 = x[:, :half], x[:, half:]
        return jnp.concatenate([x1 * cos - x2 * sin, x1 * sin + x2 * cos], axis=-1)

    for h in range(RET_HEADS):
        log_gamma = math.log(1.0 - 2.0 ** (-5.0 - h))
        sl = slice(h * RET_DV, (h + 1) * RET_DV)
        dec = jnp.exp(jnp.where(row >= col, lag * log_gamma, -jnp.inf))
        y_rows = []
        for sq in range(nb):
            ro = sq * c
            q = rot(p_ref[ro:ro + c, h * RET_DK:(h + 1) * RET_DK])
            k = rot(p_ref[ro:ro + c, qk + h * RET_DK:qk + (h + 1) * RET_DK]) * (RET_DK ** -0.5)
            v = p_ref[ro:ro + c, 2 * qk + h * RET_DV:2 * qk + (h + 1) * RET_DV]
            gate = p_ref[ro:ro + c, 2 * qk + vw + h * RET_DV:2 * qk + vw + (h + 1) * RET_DV]
            s = s_scr[sq, h]
            att = _dot_nt(q, k) * dec
            o = _dot(att, v) + jnp.exp((pos + 1.0) * log_gamma) * _dot(q, s)
            k_dec = k * jnp.exp((c - 1.0 - pos) * log_gamma)
            s_scr[sq, h] = math.exp(c * log_gamma) * s + _dot_tn(k_dec, v)

            mu = jnp.mean(o, axis=-1, keepdims=True)
            d = o - mu
            var = jnp.mean(d * d, axis=-1, keepdims=True)
            y_rows.append(d * lax.rsqrt(var + LN_EPS) * ng_ref[:, sl] * jax.nn.silu(gate))
        y_ref[:, sl] = (jnp.concatenate(y_rows, axis=0) if nb > 1 else y_rows[0]).astype(y_ref.dtype)

    @pl.when(ci == n_chunks - 1)
    def _():
        so_ref[...] = s_scr[...]


def _mamba_stack_width(c):
    return max(M_HPG * c, LANES)


def _mamba_tables(c):
    w = _mamba_stack_width(c)
    head = jnp.arange(LANES)[:, None]
    ch = jnp.arange(M_DI)[None, :]
    expand_ch = (ch // M_HEADDIM == head).astype(BF16)
    q = jnp.arange(M_GROUPS * w)[None, :]
    r = q % w
    valid = r < M_HPG * c
    expand_k = ((head == (q // w) * M_HPG + r // c) & valid).astype(BF16)
    i = jnp.arange(c)[:, None]
    key = r % c
    neg = jnp.where((i >= key) & valid, 0.0, -jnp.inf).astype(F32)
    diag = ((i == key) & valid).astype(F32)
    return expand_ch, expand_ch.T, expand_k, neg, diag


def _mamba_kernel(p_ref, cw_ref, cb_ref, dtb_ref, alog_ref, dskip_ref, ng_ref, ech_ref, echt_ref, ek_ref, neg_ref,
                  diag_ref, s0_ref, c0_ref, *rest, c, nb, n_chunks, n_alias):
    y_ref, so_ref, co_ref, s_scr, x_scr, xbc_scr = rest[n_alias:n_alias + 6]
    y_scr = rest[n_alias + 6] if nb > 1 else None
    first_row = lax.broadcasted_iota(jnp.int32, (c, 1), 0) == 0
    ci = pl.program_id(1)
    keep = M_CONV - 1
    gn = M_GROUPS * M_DSTATE
    w = _mamba_stack_width(c)
    used = M_HPG * c

    @pl.when(ci == 0)
    def _():
        for sq in range(nb):
            for grp in range(M_GROUPS):
                s_scr[sq, grp] = s0_ref[sq, grp * M_HPG:(grp + 1) * M_HPG].reshape(M_GROUP_W, M_DSTATE)
            r0, r1, r2 = c0_ref[sq, 0:1], c0_ref[sq, 1:2], c0_ref[sq, 2:3]
            x_scr[sq, 0:1, :] = cw_ref[0:1, :] * r2
            x_scr[sq, 1:2, :] = cw_ref[1:2, :] * r2 + cw_ref[0:1, :] * r1
            x_scr[sq, 2:3, :] = cw_ref[2:3, :] * r2 + cw_ref[1:2, :] * r1 + cw_ref[0:1, :] * r0

    dt = jax.nn.softplus(p_ref[:, M_DI + M_CONV_DIM:M_DI + M_CONV_DIM + LANES] + dtb_ref[...])
    g_all = _cumsum_rows(dt * -jnp.exp(alog_ref[...]), c)
    gk_all, = _dot_exact_lhs([g_all], ek_ref[...])
    blockmask = (lax.broadcasted_iota(jnp.int32, (w, M_GROUP_W), 1) // M_HEADDIM
                 == lax.broadcasted_iota(jnp.int32, (w, M_GROUP_W), 0) // c)

    for sq in range(nb):
        ro = sq * c
        for lo in range(0, M_CONV_DIM, CONV_COLS):
            cs = slice(lo, lo + CONV_COLS)
            x = p_ref[ro:ro + c, M_DI + lo:M_DI + lo + CONV_COLS]
            u = x * cw_ref[0:1, cs]
            for wi in range(1, M_CONV):
                carry = x_scr[sq, wi - 1:wi, cs]
                x_scr[sq, wi - 1:wi, cs] = u[c - 1:c]
                u = x * cw_ref[wi:wi + 1, cs] + jnp.where(first_row, carry, pltpu.roll(u, 1, 0))
            xbc_scr[:, cs] = jax.nn.silu(u + cb_ref[:, cs])

        @pl.when(ci == n_chunks - 1)
        def _():
            co_ref[sq] = p_ref[ro + c - keep:ro + c, M_DI:M_DI + M_CONV_DIM]

        g = g_all[ro:ro + c]
        g_last = g[c - 1:c]
        dt_x, eg_x, kf_x = _dot_exact_lhs([dt[ro:ro + c], jnp.exp(g), jnp.exp(g_last - g)], ech_ref[...], n_parts=2)
        egl_rows = jnp.exp(jnp.broadcast_to(g_last, (LANES, LANES)).T)
        egl_col = _dot_exact_rhs(echt_ref[...], egl_rows, n_parts=2)
        gk = gk_all[ro:ro + c]
        g_row = jnp.sum(gk * diag_ref[...], axis=0, keepdims=True)
        dec_all = jnp.exp(gk - g_row + neg_ref[...])

        for grp in range(M_GROUPS):
            gsl = slice(grp * M_GROUP_W, (grp + 1) * M_GROUP_W)
            xs = xbc_scr[:, gsl]
            bmat = xbc_scr[:, M_DI + grp * M_DSTATE:M_DI + (grp + 1) * M_DSTATE]
            cmat = xbc_scr[:, M_DI + gn + grp * M_DSTATE:M_DI + gn + (grp + 1) * M_DSTATE]
            xdt = xs * dt_x[:, gsl]
            s_t = s_scr[sq, grp]
            pad_rows = [jnp.zeros((w - used, M_DSTATE), F32)] if w > used else []
            b_cat = jnp.concatenate([bmat] * M_HPG + pad_rows, axis=0)
            pad_rows = [jnp.zeros((w - used, M_GROUP_W), F32)] if w > used else []
            x_cat = jnp.where(blockmask, jnp.concatenate([xdt] * M_HPG + pad_rows, axis=0), 0.0)
            att = _dot_nt(cmat, b_cat) * dec_all[:, grp * w:(grp + 1) * w]
            o = _dot(att, x_cat) + eg_x[:, gsl] * _dot_nt(cmat, s_t)
            s_scr[sq, grp] = egl_col[gsl] * s_t + _dot_tn(xdt * kf_x[:, gsl], bmat)

            y = (o + dskip_ref[:, gsl] * xs) * jax.nn.silu(p_ref[ro:ro + c, gsl])
            ms = jnp.mean(y * y, axis=-1, keepdims=True)
            y = y * lax.rsqrt(ms + LN_EPS) * ng_ref[:, gsl]
            if nb > 1:
                y_scr[ro:ro + c, gsl] = y
            else:
                y_ref[:, gsl] = y.astype(y_ref.dtype)

    if nb > 1:
        y_ref[...] = y_scr[...].astype(y_ref.dtype)

    @pl.when(ci == n_chunks - 1)
    def _():
        for sq in range(nb):
            for grp in range(M_GROUPS):
                so_ref[sq, grp * M_HPG:(grp + 1) * M_HPG] = s_scr[sq, grp].reshape(M_HPG, M_HEADDIM, M_DSTATE)


def _rope_tables(pos):
    half = RET_DK // 2
    inv_freq = ROPE_BASE ** (-jnp.arange(half, dtype=F32) / half)
    ang = pos.astype(F32)[:, None] * inv_freq[None, :]
    return jnp.cos(ang), jnp.sin(ang)


def _row_tile(n_rows):
    for cand in range(min(MAX_ROW_TILE, n_rows) // BF16_ROWS * BF16_ROWS, 0, -BF16_ROWS):
        if n_rows % cand == 0:
            return cand
    raise ValueError(f"no row tile for {n_rows} rows")


def kernel(x_prompt, x_sample, state_hgrn, state_ret, state_ssm, state_conv, meta_tokens, ln_g, ln_b,
           ffn_w_gate, ffn_w_up, ffn_w_down, hg_lb_logits, hg_w_in, hg_norm_g, hg_w_o,
           ret_w_in, ret_norm_g, ret_w_o, m_w_in, m_conv_w, m_conv_b, m_dt_bias, m_a_log, m_d,
           m_norm_g, m_w_o):
    bp, sp, d = x_prompt.shape
    bs, ss, _ = x_sample.shape
    n_p, n_s = bp * sp, bs * ss
    n_rows = n_p + n_s + N_META
    tm = _row_tile(n_rows)
    tm_edge = math.gcd(math.gcd(n_p, n_s), 512)
    c_s = math.gcd(ss, CHUNK)
    nb_s = BF16_ROWS // c_s if (c_s < BF16_ROWS and ss == c_s) else 1
    c_m = math.gcd(N_META, CHUNK)

    def groups(c_prompt):
        return {"meta": dict(row0=n_p + n_s, bsz=1, t=N_META, c=c_m, nb=1, shared_state=True),
                "prompt": dict(row0=0, bsz=bp, t=sp, c=c_prompt, nb=1, shared_state=True),
                "sample": dict(row0=n_p, bsz=bs, t=ss, c=c_s, nb=nb_s, shared_state=False)}

    bsz_of = {"prompt": bp, "sample": bs}
    ffn = (ffn_w_gate.astype(BF16), ffn_w_up.astype(BF16), ffn_w_down.astype(BF16),
           ln_g.reshape(DEPTH, 3, 1, d), ln_b.reshape(DEPTH, 3, 1, d))

    def run_mixer(kernel_fn, name, p, grp_cfg, consts_of, states_in, layer, y_width, state_shapes, n_slots,
                  scratch_of, bufs):
        y = None
        meta_states = None
        new_bufs = {}
        for grp in ("meta", "prompt", "sample"):
            gk = grp_cfg[grp]
            if grp == "meta":
                states0 = [jnp.zeros((1, 1) + shp, F32) for shp in state_shapes]
                st_shapes = [jax.ShapeDtypeStruct((1, 1) + shp, F32) for shp in state_shapes]
                st_bufs, st_layer, g_slot = [None] * len(state_shapes), 0, 0
            else:
                states0, st_layer = (meta_states, 0) if grp == "prompt" else (states_in, layer)
                st_shapes = [jax.ShapeDtypeStruct((n_slots, bsz_of[grp]) + shp, F32) for shp in state_shapes]
                st_bufs = bufs[grp] if bufs is not None else [None] * len(state_shapes)
                g_slot = layer
            outs = _run_mixer(
                functools.partial(kernel_fn, c=gk["c"], nb=gk["nb"], n_chunks=gk["t"] // gk["c"]), name + "_" + grp,
                p, consts_of(grp, gk), states0, st_layer, [jax.ShapeDtypeStruct((n_rows, y_width), BF16)] + st_shapes,
                [y] + st_bufs, scratch_of(gk), y_width=y_width, state_shapes=state_shapes, slot=g_slot, **gk)
            y = outs[0]
            if grp == "meta":
                meta_states = list(outs[1:])
            else:
                new_bufs[grp] = list(outs[1:])
        return y, new_bufs

    hg_bufs = ret_bufs = ssm_bufs = None
    n_hg, n_ret, n_ssm = state_hgrn.shape[0], state_ret.shape[0], state_ssm.shape[0]
    hg_w_in_b, ret_w_in_b = hg_w_in.astype(BF16), ret_w_in.astype(BF16)
    m_w_in_b = jnp.pad(m_w_in, ((0, 0), (0, 0), (0, M_COLS_PAD - M_COLS))).astype(BF16)
    hg_w_o_b, ret_w_o_b, m_w_o_b = hg_w_o.astype(BF16), ret_w_o.astype(BF16), m_w_o.astype(BF16)
    ln_g4, ln_b4 = ffn[3], ffn[4]
    state_ssm_t = jnp.swapaxes(state_ssm, 3, 4)
    h = None
    y_prompt = y_sample = None
    for i in range(DEPTH):
        if i == 0:
            h = _ffn_ln_first(x_prompt.reshape(n_p, d), x_sample.reshape(n_s, d), meta_tokens.astype(F32), ffn, i, 0, tm_edge)
        else:
            h = _ffn_ln(h, ffn, i, 0, tm)
        kind, j = i % 3, i // 3
        if kind == 0:
            p = _proj(h, hg_w_in_b, j, tm)
            shp = (HG_HEADS, HG_DK, HG_DV)
            y, hg_bufs = run_mixer(
                functools.partial(_hgrn_kernel, layer=i), "hgrn", p, groups(math.gcd(sp, CHUNK)),
                lambda grp, gk: [hg_lb_logits, hg_norm_g[j].reshape(1, -1)],
                [state_hgrn], j, HG_HEADS * HG_DV, [shp], n_hg,
                lambda gk: [pltpu.VMEM((gk["nb"],) + shp, F32)], hg_bufs)
            w_o = hg_w_o_b
        elif kind == 1:
            p = _proj(h, ret_w_in_b, j, tm)
            shp = (RET_HEADS, RET_DK, RET_DV)
            grp_cfg = groups(math.gcd(sp, RET_CHUNK))
            rope = {"meta": _rope_tables(jnp.arange(N_META)), "prompt": _rope_tables(N_META + jnp.arange(sp)),
                    "sample": _rope_tables(PAST_LEN + jnp.arange(ss))}
            y, ret_bufs = run_mixer(
                _ret_kernel, "retention", p, grp_cfg,
                lambda grp, gk: [rope[grp][0], rope[grp][1], ret_norm_g[j].reshape(1, -1)],
                [state_ret], j, RET_HEADS * RET_DV, [shp], n_ret,
                lambda gk: [pltpu.VMEM((gk["nb"],) + shp, F32)], ret_bufs)
            w_o = ret_w_o_b
        else:
            p = _proj(h, m_w_in_b, j, tm)
            shp, cshp = (M_HEADS, M_HEADDIM, M_DSTATE), (M_CONV - 1, M_CONV_DIM)
            pad = LANES - M_HEADS
            consts = [m_conv_w[j], m_conv_b[j].reshape(1, -1), jnp.pad(m_dt_bias[j], (0, pad)).reshape(1, LANES),
                      jnp.pad(m_a_log[j], (0, pad)).reshape(1, LANES), jnp.repeat(m_d[j], M_HEADDIM).reshape(1, -1),
                      m_norm_g[j].reshape(1, -1)]
            tables = {}
            y, ssm_bufs = run_mixer(
                _mamba_kernel, "mamba", p, groups(math.gcd(sp, CHUNK)),
                lambda grp, gk: consts + list(tables.setdefault(gk["c"], _mamba_tables(gk["c"]))),
                [state_ssm_t, state_conv], j, M_DI, [shp, cshp], n_ssm,
                lambda gk: [pltpu.VMEM((gk["nb"], M_GROUPS, M_GROUP_W, M_DSTATE), F32),
                            pltpu.VMEM((gk["nb"], 8, M_CONV_DIM), F32), pltpu.VMEM((gk["c"], M_CONV_DIM), F32)]
                + ([pltpu.VMEM((gk["nb"] * gk["c"], M_DI), F32)] if gk["nb"] > 1 else []), ssm_bufs)
            w_o = m_w_o_b
        if i < DEPTH - 1:
            h = _out_ln(h, y, w_o, j, ln_g4, ln_b4, i, tm)
            h = _ffn_ln(h, ffn, i, 1, tm)
        else:
            h = _out_ln(h, y, w_o, j, ln_g4, ln_b4, i, tm_edge, n_rows_out=n_p + n_s)
            y_prompt, y_sample = _ffn_ln_last(h, n_p, n_s, ffn, i, 1, tm_edge)

    return (y_prompt.reshape(bp, sp, d), y_sample.reshape(bs, ss, d),
            hg_bufs["prompt"][0], hg_bufs["sample"][0], ret_bufs["prompt"][0], ret_bufs["sample"][0],
            jnp.swapaxes(ssm_bufs["prompt"][0], 3, 4), jnp.swapaxes(ssm_bufs["sample"][0], 3, 4),
            ssm_bufs["prompt"][1], ssm_bufs["sample"][1])
```

```python
import functools
import math

import jax
import jax.numpy as jnp
from jax import lax
from jax.experimental import pallas as pl
from jax.experimental.pallas import tpu as pltpu

F32 = jnp.float32
BF16 = jnp.bfloat16

D_MODEL = 1024
DEPTH = 4
N_META = 16
PAST_LEN = 16384
CHUNK = 64
ALPHA = (2 * DEPTH) ** 0.25
LN_EPS = 1e-5
ROPE_BASE = 10000.0
HG_HEADS, HG_DK, HG_DV = 8, 128, 128
RET_HEADS, RET_DK, RET_DV = 4, 256, 512
RET_CHUNK = 128
M_DI, M_HEADDIM, M_HEADS, M_GROUPS, M_DSTATE, M_CONV = 2048, 64, 32, 8, 128, 4
M_CONV_DIM = M_DI + 2 * M_GROUPS * M_DSTATE
M_GROUP_W = M_DI // M_GROUPS
M_HPG = M_HEADS // M_GROUPS
M_COLS = M_DI + M_CONV_DIM + M_HEADS
LANES = 128
BF16_ROWS = 16
M_COLS_PAD = ((M_COLS + LANES - 1) // LANES) * LANES
CONV_COLS = 512
VMEM_LIMIT = 56 * 1024 * 1024
MAX_ROW_TILE = 528
LOG2E = 1.4426950408889634


def _params(n_grid, flags=None):
    return pltpu.CompilerParams(dimension_semantics=("arbitrary",) * n_grid, vmem_limit_bytes=VMEM_LIMIT, flags=flags)


def _resident(shape):
    return pl.BlockSpec(shape, lambda *_: (0,) * len(shape), pipeline_mode=pl.Buffered(1))


def _resident_at(shape, lead):
    return pl.BlockSpec((None,) * len(lead) + tuple(shape), lambda *_: tuple(lead) + (0,) * len(shape),
                        pipeline_mode=pl.Buffered(1))


def _dot(a, b):
    return jnp.dot(a.astype(BF16), b.astype(BF16), preferred_element_type=F32)


def _dot_nt(a, b):
    return lax.dot_general(a.astype(BF16), b.astype(BF16), (((1,), (1,)), ((), ())), preferred_element_type=F32)


def _dot_tn(a, b):
    return lax.dot_general(a.astype(BF16), b.astype(BF16), (((0,), (0,)), ((), ())), preferred_element_type=F32)


def _split(x, n_parts):
    parts = []
    r = x
    for _ in range(n_parts - 1):
        hi = r.astype(BF16).astype(F32)
        parts.append(hi)
        r = r - hi
    parts.append(r)
    return parts


def _dot_exact_rhs(a01, x, n_parts=3):
    n = x.shape[1]
    out = jnp.dot(a01, jnp.concatenate(_split(x, n_parts), axis=1).astype(BF16), preferred_element_type=F32)
    res = out[:, :n]
    for i in range(1, n_parts):
        res = res + out[:, i * n:(i + 1) * n]
    return res


def _dot_exact_lhs(xs, b01, n_parts=3):
    r = xs[0].shape[0]
    parts = [p for x in xs for p in _split(x, n_parts)]
    out = jnp.dot(jnp.concatenate(parts, axis=0).astype(BF16), b01, preferred_element_type=F32)
    res = []
    for i in range(len(xs)):
        acc = out[n_parts * i * r:(n_parts * i + 1) * r]
        for k in range(1, n_parts):
            acc = acc + out[(n_parts * i + k) * r:(n_parts * i + k + 1) * r]
        res.append(acc)
    return res


def _cumsum_rows(x, c):
    n = x.shape[0]
    row = lax.broadcasted_iota(jnp.int32, (n, n), 0)
    col = lax.broadcasted_iota(jnp.int32, (n, n), 1)
    tri = (row >= col) if n == c else ((row >= col) & (row // c == col // c))
    return _dot_exact_rhs(jnp.where(tri, 1.0, 0.0).astype(BF16), x)


def _layer_norm(y, g, b):
    mu = jnp.mean(y, axis=-1, keepdims=True)
    d = y - mu
    var = jnp.mean(d * d, axis=-1, keepdims=True)
    return d * lax.rsqrt(var + LN_EPS) * g + b


def _ffn_ln_kernel(*refs, n_src, n_dst, tiles, ff_chunk):
    srcs, (wg_ref, wu_ref, wd_ref, g_ref, b_ref) = refs[:n_src], refs[n_src:n_src + 5]
    dsts = refs[n_src + 5:n_src + 5 + n_dst]
    i = pl.program_id(0)
    if n_src == 1:
        x = srcs[0][...]
    else:
        x_scr = refs[-1]
        n_meta = srcs[2].shape[0]

        @pl.when(i < tiles[0])
        def _():
            x_scr[...] = srcs[0][...]

        @pl.when((i >= tiles[0]) & (i < tiles[0] + tiles[1]))
        def _():
            x_scr[...] = srcs[1][...]

        @pl.when(i == tiles[0] + tiles[1])
        def _():
            x_scr[0:n_meta, :] = srcs[2][...]
            if x_scr.shape[0] > n_meta:
                x_scr[n_meta:, :] = jnp.zeros((x_scr.shape[0] - n_meta, x_scr.shape[1]), F32)

        x = x_scr[...]
    xb = x.astype(BF16)
    d_ff = wg_ref.shape[1]
    acc = jnp.zeros(x.shape, F32)
    for lo in range(0, d_ff, ff_chunk):
        gate = jnp.dot(xb, wg_ref[:, lo:lo + ff_chunk], preferred_element_type=F32)
        up = jnp.dot(xb, wu_ref[:, lo:lo + ff_chunk], preferred_element_type=F32)
        act = (jax.nn.silu(gate) * up).astype(BF16)
        acc = acc + jnp.dot(act, wd_ref[lo:lo + ff_chunk, :], preferred_element_type=F32)
    res = _layer_norm(ALPHA * x + 0.5 * acc, g_ref[...], b_ref[...])
    if n_dst == 1:
        dsts[0][...] = res
    else:
        @pl.when(i < tiles[0])
        def _():
            dsts[0][...] = res

        @pl.when(i >= tiles[0])
        def _():
            dsts[1][...] = res


def _ffn_weights(ffn, layer, which):
    wg, wu, wd, ln_g, ln_b = ffn
    d, d_ff = wg.shape[2:]
    ln_row = 0 if which == 0 else 2
    specs = [_resident_at((d, d_ff), (layer, which)), _resident_at((d, d_ff), (layer, which)),
             _resident_at((d_ff, d), (layer, which)), _resident_at((1, d), (layer, ln_row)),
             _resident_at((1, d), (layer, ln_row))]
    return specs, (wg, wu, wd, ln_g, ln_b), d_ff // 11


def _ffn_ln(x, ffn, layer, which, tm):
    m, d = x.shape
    w_specs, w_args, ff_chunk = _ffn_weights(ffn, layer, which)
    return pl.pallas_call(
        functools.partial(_ffn_ln_kernel, n_src=1, n_dst=1, tiles=None, ff_chunk=ff_chunk),
        grid=(m // tm,),
        in_specs=[pl.BlockSpec((tm, d), lambda i: (i, 0))] + w_specs,
        out_specs=pl.BlockSpec((tm, d), lambda i: (i, 0)),
        out_shape=jax.ShapeDtypeStruct((m, d), F32),
        compiler_params=_params(1), name="ffn_ln",
    )(x, *w_args)


def _ffn_ln_first(x_p, x_s, x_m, ffn, layer, which, tm):
    (n_p, d), n_s, n_m = x_p.shape, x_s.shape[0], x_m.shape[0]
    tp, ts = n_p // tm, n_s // tm
    w_specs, w_args, ff_chunk = _ffn_weights(ffn, layer, which)
    return pl.pallas_call(
        functools.partial(_ffn_ln_kernel, n_src=3, n_dst=1, tiles=(tp, ts), ff_chunk=ff_chunk),
        grid=(tp + ts + 1,),
        in_specs=[pl.BlockSpec((tm, d), lambda i: (jnp.minimum(i, tp - 1), 0)),
                  pl.BlockSpec((tm, d), lambda i: (jnp.clip(i - tp, 0, ts - 1), 0)),
                  pl.BlockSpec((n_m, d), lambda i: (0, 0))] + w_specs,
        out_specs=pl.BlockSpec((tm, d), lambda i: (i, 0)),
        out_shape=jax.ShapeDtypeStruct((n_p + n_s + n_m, d), F32),
        scratch_shapes=[pltpu.VMEM((tm, d), F32)],
        compiler_params=_params(1), name="ffn_ln_first",
    )(x_p, x_s, x_m, *w_args)


def _ffn_ln_last(x, n_p, n_s, ffn, layer, which, tm):
    d = x.shape[1]
    tp, ts = n_p // tm, n_s // tm
    w_specs, w_args, ff_chunk = _ffn_weights(ffn, layer, which)
    return pl.pallas_call(
        functools.partial(_ffn_ln_kernel, n_src=1, n_dst=2, tiles=(tp, ts), ff_chunk=ff_chunk),
        grid=(tp + ts,),
        in_specs=[pl.BlockSpec((tm, d), lambda i: (i, 0))] + w_specs,
        out_specs=[pl.BlockSpec((tm, d), lambda i: (jnp.minimum(i, tp - 1), 0)),
                   pl.BlockSpec((tm, d), lambda i: (jnp.clip(i - tp, 0, ts - 1), 0))],
        out_shape=[jax.ShapeDtypeStruct((n_p, d), F32), jax.ShapeDtypeStruct((n_s, d), F32)],
        compiler_params=_params(1), name="ffn_ln_last",
    )(x, *w_args)


def _proj_kernel(x_ref, w_ref, o_ref, *, col_chunk):
    xb = x_ref[...].astype(BF16)
    n = w_ref.shape[1]
    for lo in range(0, n, col_chunk):
        hi = min(lo + col_chunk, n)
        o_ref[:, lo:hi] = jnp.dot(xb, w_ref[:, lo:hi], preferred_element_type=F32)


def _proj(x, w, layer, tm):
    m, d = x.shape
    n = w.shape[2]
    return pl.pallas_call(
        functools.partial(_proj_kernel, col_chunk=512),
        grid=(m // tm,),
        in_specs=[pl.BlockSpec((tm, d), lambda i: (i, 0)), _resident_at((d, n), (layer,))],
        out_specs=pl.BlockSpec((tm, n), lambda i: (i, 0)),
        out_shape=jax.ShapeDtypeStruct((m, n), F32),
        compiler_params=_params(1), name="proj",
    )(x, w)


def _out_ln_kernel(h_ref, y_ref, w_ref, g_ref, b_ref, o_ref):
    m = jnp.dot(y_ref[...], w_ref[...], preferred_element_type=F32)
    o_ref[...] = _layer_norm(ALPHA * h_ref[...] + m, g_ref[...], b_ref[...])


def _out_ln(h, y, w, w_layer, ln_g, ln_b, layer, tm, n_rows_out=None):
    m, d = h.shape
    m_out = m if n_rows_out is None else n_rows_out
    k = y.shape[1]
    return pl.pallas_call(
        _out_ln_kernel,
        grid=(m_out // tm,),
        in_specs=[pl.BlockSpec((tm, d), lambda i: (i, 0)), pl.BlockSpec((tm, k), lambda i: (i, 0)),
                  _resident_at((k, d), (w_layer,)), _resident_at((1, d), (layer, 1)), _resident_at((1, d), (layer, 1))],
        out_specs=pl.BlockSpec((tm, d), lambda i: (i, 0)),
        out_shape=jax.ShapeDtypeStruct((m_out, d), F32),
        compiler_params=_params(1), name="out_ln",
    )(h, y, w, ln_g, ln_b)


def _run_mixer(kernel_fn, name, p, consts, states0, state_layer, out_shapes, out_bufs, scratch, *, y_width,
               state_shapes, row0, bsz, t, c, nb, shared_state, slot):
    n_chunks = t // c
    assert nb == 1 or n_chunks == 1
    rows = nb * c
    assert row0 % rows == 0 and bsz % nb == 0
    blk0 = row0 // rows
    row_idx = lambda sb, ci: (blk0 + sb * n_chunks + ci, 0)
    in_specs = [pl.BlockSpec((rows, p.shape[1]), row_idx)]
    in_specs += [pl.BlockSpec(a.shape, lambda sb, ci, _n=a.ndim: (0,) * _n) for a in consts]
    for shp in state_shapes:
        zeros = (0,) * len(shp)
        in_specs.append(pl.BlockSpec((None, nb) + shp, (lambda sb, ci, _z=zeros: (state_layer, 0) + _z) if shared_state
                                     else (lambda sb, ci, _z=zeros: (state_layer, sb) + _z)))
    out_specs = [pl.BlockSpec((rows, y_width), row_idx)]
    for shp in state_shapes:
        zeros = (0,) * len(shp)
        out_specs.append(pl.BlockSpec((None, nb) + shp, lambda sb, ci, _z=zeros: (slot, sb) + _z))
    args = [p, *consts, *states0]
    n_in = len(args)
    aliases = {}
    for k, buf in enumerate(out_bufs):
        if buf is not None:
            aliases[len(args)] = k
            in_specs.append(pl.BlockSpec(memory_space=pl.ANY))
            args.append(buf)
    return pl.pallas_call(
        functools.partial(kernel_fn, n_alias=len(args) - n_in),
        grid=(bsz // nb, n_chunks), in_specs=in_specs, out_specs=out_specs, out_shape=out_shapes,
        scratch_shapes=scratch, input_output_aliases=aliases,
        compiler_params=_params(2), name=name,
    )(*args)


def _hgrn_piece_plan(blk):
    return [(t0, u0, t0 == u0) for t0 in range(0, blk, 8) for u0 in range(0, t0 + 8, 8)]


def _hgrn_kernel(p_ref, lbl_ref, ng_ref, s0_ref, *rest, c, nb, n_chunks, layer, n_alias):
    y_ref, so_ref, s_scr = rest[n_alias:]
    ci = pl.program_id(1)

    @pl.when(ci == 0)
    def _():
        s_scr[...] = s0_ref[...]

    fw = HG_HEADS * HG_DK
    logits = lbl_ref[...]
    e = jnp.exp(logits - jnp.max(logits, axis=0, keepdims=True))
    prob = e / jnp.sum(e, axis=0, keepdims=True)
    lb = prob[0:1] * 0.0
    run = prob[0:1]
    for r in range(1, layer + 1):
        run = run + prob[r:r + 1]
        lb = run - prob[0:1]

    z = p_ref[:, fw:2 * fw]
    ez = jnp.exp(-jnp.abs(z))
    s_big = 1.0 / (1.0 + ez)
    s_small = ez * s_big
    pos = z >= 0.0
    f = lb + (1.0 - lb) * jnp.where(pos, s_big, s_small)
    k_all = (1.0 - lb) * jnp.where(pos, s_small, s_big)
    g_all = _cumsum_rows(jnp.log(f), c)
    g2_all = g_all * LOG2E
    q_all = jax.nn.silu(p_ref[:, 0:fw])
    blk = 16 if c % 16 == 0 else 8
    plan = _hgrn_piece_plan(blk)
    row8 = lax.broadcasted_iota(jnp.int32, (8, HG_DK), 0)
    ones = jnp.ones((HG_DK, HG_DV), BF16)
    if c > blk:
        n_off = sum(range(blk, c, blk))
        orow = lax.broadcasted_iota(jnp.int32, (c - blk, n_off), 0) // blk
        ocol = lax.broadcasted_iota(jnp.int32, (c - blk, n_off), 1)
        off_mask = jnp.zeros((c - blk, n_off), jnp.bool_)
        start = 0
        for b, r0 in enumerate(range(blk, c, blk)):
            off_mask = off_mask | ((orow == b) & (ocol >= start) & (ocol < start + r0))
            start += r0

    def v_of(h, ro):
        return p_ref[ro:ro + c, 2 * fw + h * HG_DV:2 * fw + (h + 1) * HG_DV]

    staged = {}
    for h in range(HG_HEADS):
        sl = slice(h * HG_DK, (h + 1) * HG_DK)
        for sq in range(nb):
            ro = sq * c
            q, k, g2 = q_all[ro:ro + c, sl], k_all[ro:ro + c, sl], g2_all[ro:ro + c, sl]
            v = v_of(h, ro)
            s = s_scr[sq, h]
            o_inter = _dot(q * jnp.exp2(g2), s)
            pieces = []
            for r0 in range(0, c, blk):
                for (t0, u0, masked) in plan:
                    qt, gt = q[r0 + t0:r0 + t0 + 8], g2[r0 + t0:r0 + t0 + 8]
                    for j in range(r0 + u0, r0 + u0 + 8):
                        diff = gt - g2[j:j + 1]
                        if masked:
                            diff = jnp.where(row8 >= (j - r0 - u0), diff, -jnp.inf)
                        pieces.append(qt * jnp.exp2(diff) * k[j:j + 1])
            summed = jnp.dot(jnp.concatenate(pieces, axis=0).astype(BF16), ones, preferred_element_type=F32)
            att = None
            if c > blk:
                q_t = jnp.concatenate([q[r0:r0 + blk] * jnp.exp2(g2[r0:r0 + blk] - g2[r0:r0 + 1])
                                       for r0 in range(blk, c, blk)], axis=0)
                k_t = jnp.concatenate([k[0:r0] * jnp.exp2(g2[r0:r0 + 1] - g2[0:r0]) for r0 in range(blk, c, blk)], axis=0)
                att = jnp.where(off_mask, _dot_nt(q_t, k_t), 0.0).astype(BF16)
            g_last = g2[c - 1:c]
            k_dec = k * jnp.exp2(g_last - g2)
            dec_col = jnp.exp2(jnp.broadcast_to(g_last, (HG_DV, HG_DK)).T)
            s_scr[sq, h] = dec_col * s + _dot_tn(k_dec, v)
            staged[h, sq] = (o_inter, summed, att)

    for h in range(HG_HEADS):
        sl = slice(h * HG_DK, (h + 1) * HG_DK)
        y_rows = []
        for sq in range(nb):
            ro = sq * c
            o_inter, summed, att = staged[h, sq]
            v = v_of(h, ro)
            gate = p_ref[ro:ro + c, 2 * fw + HG_HEADS * HG_DV + h * HG_DV:2 * fw + HG_HEADS * HG_DV + (h + 1) * HG_DV]
            o_rows = []
            idx = 0
            for r0 in range(0, c, blk):
                o_tiles = {}
                for (t0, u0, _) in plan:
                    for j in range(r0 + u0, r0 + u0 + 8):
                        term = summed[idx * 8:(idx + 1) * 8] * v[j:j + 1]
                        o_tiles[t0] = term if t0 not in o_tiles else o_tiles[t0] + term
                        idx += 1
                o_rows.extend(o_tiles[t0] for t0 in range(0, blk, 8))
            o = o_inter + (jnp.concatenate(o_rows, axis=0) if len(o_rows) > 1 else o_rows[0])
            if c > blk:
                v_t = jnp.concatenate([v[0:r0] for r0 in range(blk, c, blk)], axis=0)
                o = o + jnp.concatenate([jnp.zeros((blk, HG_DV), F32), _dot(att, v_t)], axis=0)
            ms = jnp.mean(o * o, axis=-1, keepdims=True)
            y_rows.append(o * lax.rsqrt(ms + LN_EPS) * ng_ref[:, sl] * jax.nn.silu(gate))
        y_ref[:, sl] = (jnp.concatenate(y_rows, axis=0) if nb > 1 else y_rows[0]).astype(y_ref.dtype)

    @pl.when(ci == n_chunks - 1)
    def _():
        so_ref[...] = s_scr[...]


def _ret_kernel(p_ref, cos_ref, sin_ref, ng_ref, s0_ref, *rest, c, nb, n_chunks, n_alias):
    y_ref, so_ref, s_scr = rest[n_alias:]
    ci = pl.program_id(1)

    @pl.when(ci == 0)
    def _():
        s_scr[...] = s0_ref[...]

    qk = RET_HEADS * RET_DK
    vw = RET_HEADS * RET_DV
    half = RET_DK // 2
    row = lax.broadcasted_iota(jnp.int32, (c, c), 0)
    col = lax.broadcasted_iota(jnp.int32, (c, c), 1)
    lag = (row - col).astype(F32)
    pos = lax.broadcasted_iota(jnp.int32, (c, 1), 0).astype(F32)
    cos = cos_ref[pl.ds(pl.multiple_of(ci * c, c), c), :]
    sin = sin_ref[pl.ds(pl.multiple_of(ci * c, c), c), :]

    def rot(x):
        x1, x2 = x[:, :half], x[:, half:]
        return jnp.concatenate([x1 * cos - x2 * sin, x1 * sin + x2 * cos], axis=-1)

    def v_of(h, ro):
        return p_ref[ro:ro + c, 2 * qk + h * RET_DV:2 * qk + (h + 1) * RET_DV]

    staged = {}
    for h in range(RET_HEADS):
        log_gamma = math.log(1.0 - 2.0 ** (-5.0 - h))
        dec = jnp.exp(jnp.where(row >= col, lag * log_gamma, -jnp.inf))
        for sq in range(nb):
            ro = sq * c
            q = rot(p_ref[ro:ro + c, h * RET_DK:(h + 1) * RET_DK])
            k = rot(p_ref[ro:ro + c, qk + h * RET_DK:qk + (h + 1) * RET_DK]) * (RET_DK ** -0.5)
            s = s_scr[sq, h]
            att = (_dot_nt(q, k) * dec).astype(BF16)
            o_inter = jnp.exp((pos + 1.0) * log_gamma) * _dot(q, s)
            k_dec = k * jnp.exp((c - 1.0 - pos) * log_gamma)
            s_scr[sq, h] = math.exp(c * log_gamma) * s + _dot_tn(k_dec, v_of(h, ro))
            staged[h, sq] = (att, o_inter)

    for h in range(RET_HEADS):
        sl = slice(h * RET_DV, (h + 1) * RET_DV)
        y_rows = []
        for sq in range(nb):
            ro = sq * c
            att, o_inter = staged[h, sq]
            gate = p_ref[ro:ro + c, 2 * qk + vw + h * RET_DV:2 * qk + vw + (h + 1) * RET_DV]
            o = _dot(att, v_of(h, ro)) + o_inter

            mu = jnp.mean(o, axis=-1, keepdims=True)
            d = o - mu
            var = jnp.mean(d * d, axis=-1, keepdims=True)
            y_rows.append(d * lax.rsqrt(var + LN_EPS) * ng_ref[:, sl] * jax.nn.silu(gate))
        y_ref[:, sl] = (jnp.concatenate(y_rows, axis=0) if nb > 1 else y_rows[0]).astype(y_ref.dtype)

    @pl.when(ci == n_chunks - 1)
    def _():
        so_ref[...] = s_scr[...]


def _mamba_stack_width(c):
    return max(M_HPG * c, LANES)


def _mamba_tables(c):
    w = _mamba_stack_width(c)
    head = jnp.arange(LANES)[:, None]
    ch = jnp.arange(M_DI)[None, :]
    expand_ch = (ch // M_HEADDIM == head).astype(BF16)
    q = jnp.arange(M_GROUPS * w)[None, :]
    r = q % w
    valid = r < M_HPG * c
    expand_k = ((head == (q // w) * M_HPG + r // c) & valid).astype(BF16)
    i = jnp.arange(c)[:, None]
    key = r % c
    neg = jnp.where((i >= key) & valid, 0.0, -jnp.inf).astype(F32)
    diag = ((i == key) & valid).astype(F32)
    return expand_ch, expand_ch.T, expand_k, neg, diag


def _mamba_kernel(p_ref, cw_ref, cb_ref, dtb_ref, alog_ref, dskip_ref, ng_ref, ech_ref, echt_ref, ek_ref, neg_ref,
                  diag_ref, s0_ref, c0_ref, *rest, c, nb, n_chunks, n_alias):
    y_ref, so_ref, co_ref, s_scr, x_scr, xbc_scr = rest[n_alias:n_alias + 6]
    y_scr = rest[n_alias + 6] if nb > 1 else None
    first_row = lax.broadcasted_iota(jnp.int32, (8, 1), 0) == 0
    ci = pl.program_id(1)
    keep = M_CONV - 1
    gn = M_GROUPS * M_DSTATE
    w = _mamba_stack_width(c)
    used = M_HPG * c

    @pl.when(ci == 0)
    def _():
        for sq in range(nb):
            for grp in range(M_GROUPS):
                s_scr[sq, grp] = s0_ref[sq, grp * M_HPG:(grp + 1) * M_HPG].reshape(M_GROUP_W, M_DSTATE)
            r0, r1, r2 = c0_ref[sq, 0:1], c0_ref[sq, 1:2], c0_ref[sq, 2:3]
            x_scr[sq, 0:1, :] = cw_ref[0:1, :] * r2
            x_scr[sq, 1:2, :] = cw_ref[1:2, :] * r2 + cw_ref[0:1, :] * r1
            x_scr[sq, 2:3, :] = cw_ref[2:3, :] * r2 + cw_ref[1:2, :] * r1 + cw_ref[0:1, :] * r0

    dt = jax.nn.softplus(p_ref[:, M_DI + M_CONV_DIM:M_DI + M_CONV_DIM + LANES] + dtb_ref[...])
    g_all = _cumsum_rows(dt * -jnp.exp(alog_ref[...]), c)
    gk_all, = _dot_exact_lhs([g_all], ek_ref[...])
    blockmask = (lax.broadcasted_iota(jnp.int32, (w, M_GROUP_W), 1) // M_HEADDIM
                 == lax.broadcasted_iota(jnp.int32, (w, M_GROUP_W), 0) // c)

    for sq in range(nb):
        ro = sq * c
        for lo in range(0, M_CONV_DIM, CONV_COLS):
            cs = slice(lo, lo + CONV_COLS)
            x = p_ref[ro:ro + c, M_DI + lo:M_DI + lo + CONV_COLS]
            u = x * cw_ref[0:1, cs]
            for wi in range(1, M_CONV):
                carry = x_scr[sq, wi - 1:wi, cs]
                x_scr[sq, wi - 1:wi, cs] = u[c - 1:c]
                rolled = pltpu.roll(u, 1, 0)
                head_rows = jnp.where(first_row, carry, rolled[0:8])
                shifted = jnp.concatenate([head_rows, rolled[8:]], axis=0) if c > 8 else head_rows
                u = x * cw_ref[wi:wi + 1, cs] + shifted
            xbc_scr[:, cs] = jax.nn.silu(u + cb_ref[:, cs])

        @pl.when(ci == n_chunks - 1)
        def _():
            co_ref[sq] = p_ref[ro + c - keep:ro + c, M_DI:M_DI + M_CONV_DIM]

        g = g_all[ro:ro + c]
        g_last = g[c - 1:c]
        dt_x, eg_x, kf_x = _dot_exact_lhs([dt[ro:ro + c], jnp.exp(g), jnp.exp(g_last - g)], ech_ref[...], n_parts=2)
        egl_rows = jnp.exp(jnp.broadcast_to(g_last, (LANES, LANES)).T)
        egl_col = _dot_exact_rhs(echt_ref[...], egl_rows, n_parts=2)
        gk = gk_all[ro:ro + c]
        g_row = jnp.sum(gk * diag_ref[...], axis=0, keepdims=True)
        dec_all = jnp.exp(gk - g_row + neg_ref[...])

        atts = []
        for grp in range(M_GROUPS):
            bmat = xbc_scr[:, M_DI + grp * M_DSTATE:M_DI + (grp + 1) * M_DSTATE]
            cmat = xbc_scr[:, M_DI + gn + grp * M_DSTATE:M_DI + gn + (grp + 1) * M_DSTATE]
            pad_rows = [jnp.zeros((w - used, M_DSTATE), F32)] if w > used else []
            b_cat = jnp.concatenate([bmat] * M_HPG + pad_rows, axis=0)
            atts.append((_dot_nt(cmat, b_cat) * dec_all[:, grp * w:(grp + 1) * w]).astype(BF16))

        for grp in range(M_GROUPS):
            gsl = slice(grp * M_GROUP_W, (grp + 1) * M_GROUP_W)
            xs = xbc_scr[:, gsl]
            bmat = xbc_scr[:, M_DI + grp * M_DSTATE:M_DI + (grp + 1) * M_DSTATE]
            cmat = xbc_scr[:, M_DI + gn + grp * M_DSTATE:M_DI + gn + (grp + 1) * M_DSTATE]
            xdt = xs * dt_x[:, gsl]
            s_t = s_scr[sq, grp]
            pad_rows = [jnp.zeros((w - used, M_GROUP_W), F32)] if w > used else []
            x_cat = jnp.where(blockmask, jnp.concatenate([xdt] * M_HPG + pad_rows, axis=0), 0.0)
            o = _dot(atts[grp], x_cat) + eg_x[:, gsl] * _dot_nt(cmat, s_t)
            s_scr[sq, grp] = egl_col[gsl] * s_t + _dot_tn(xdt * kf_x[:, gsl], bmat)

            y = (o + dskip_ref[:, gsl] * xs) * jax.nn.silu(p_ref[ro:ro + c, gsl])
            ms = jnp.mean(y * y, axis=-1, keepdims=True)
            y = y * lax.rsqrt(ms + LN_EPS) * ng_ref[:, gsl]
            if nb > 1:
                y_scr[ro:ro + c, gsl] = y
            else:
                y_ref[:, gsl] = y.astype(y_ref.dtype)

    if nb > 1:
        y_ref[...] = y_scr[...].astype(y_ref.dtype)

    @pl.when(ci == n_chunks - 1)
    def _():
        for sq in range(nb):
            for grp in range(M_GROUPS):
                so_ref[sq, grp * M_HPG:(grp + 1) * M_HPG] = s_scr[sq, grp].reshape(M_HPG, M_HEADDIM, M_DSTATE)


def _rope_tables(pos):
    half = RET_DK // 2
    inv_freq = ROPE_BASE ** (-jnp.arange(half, dtype=F32) / half)
    ang = pos.astype(F32)[:, None] * inv_freq[None, :]
    return jnp.cos(ang), jnp.sin(ang)


def _row_tile(n_rows):
    for cand in range(min(MAX_ROW_TILE, n_rows) // BF16_ROWS * BF16_ROWS, 0, -BF16_ROWS):
        if n_rows % cand == 0:
            return cand
    raise ValueError(f"no row tile for {n_rows} rows")


def kernel(x_prompt, x_sample, state_hgrn, state_ret, state_ssm, state_conv, meta_tokens, ln_g, ln_b,
           ffn_w_gate, ffn_w_up, ffn_w_down, hg_lb_logits, hg_w_in, hg_norm_g, hg_w_o,
           ret_w_in, ret_norm_g, ret_w_o, m_w_in, m_conv_w, m_conv_b, m_dt_bias, m_a_log, m_d,
           m_norm_g, m_w_o):
    bp, sp, d = x_prompt.shape
    bs, ss, _ = x_sample.shape
    n_p, n_s = bp * sp, bs * ss
    n_rows = n_p + n_s + N_META
    tm = _row_tile(n_rows)
    tm_edge = math.gcd(math.gcd(n_p, n_s), 512)
    c_s = math.gcd(ss, CHUNK)
    c_m = math.gcd(N_META, CHUNK)

    def groups(c_prompt, sample_rows):
        nb_s = 1
        if ss == c_s:
            nb_s = max(1, math.gcd(bs, max(sample_rows, BF16_ROWS) // c_s))
        return {"meta": dict(row0=n_p + n_s, bsz=1, t=N_META, c=c_m, nb=1, shared_state=True),
                "prompt": dict(row0=0, bsz=bp, t=sp, c=c_prompt, nb=1, shared_state=True),
                "sample": dict(row0=n_p, bsz=bs, t=ss, c=c_s, nb=nb_s, shared_state=False)}

    bsz_of = {"prompt": bp, "sample": bs}
    ffn = (ffn_w_gate.astype(BF16), ffn_w_up.astype(BF16), ffn_w_down.astype(BF16),
           ln_g.reshape(DEPTH, 3, 1, d), ln_b.reshape(DEPTH, 3, 1, d))

    def run_mixer(kernel_fn, name, p, grp_cfg, consts_of, states_in, layer, y_width, state_shapes, n_slots,
                  scratch_of, bufs):
        y = None
        meta_states = None
        new_bufs = {}
        for grp in ("meta", "prompt", "sample"):
            gk = grp_cfg[grp]
            if grp == "meta":
                states0 = [jnp.zeros((1, 1) + shp, F32) for shp in state_shapes]
                st_shapes = [jax.ShapeDtypeStruct((1, 1) + shp, F32) for shp in state_shapes]
                st_bufs, st_layer, g_slot = [None] * len(state_shapes), 0, 0
            else:
                states0, st_layer = (meta_states, 0) if grp == "prompt" else (states_in, layer)
                st_shapes = [jax.ShapeDtypeStruct((n_slots, bsz_of[grp]) + shp, F32) for shp in state_shapes]
                st_bufs = bufs[grp] if bufs is not None else [None] * len(state_shapes)
                g_slot = layer
            outs = _run_mixer(
                functools.partial(kernel_fn, c=gk["c"], nb=gk["nb"], n_chunks=gk["t"] // gk["c"]), name + "_" + grp,
                p, consts_of(grp, gk), states0, st_layer, [jax.ShapeDtypeStruct((n_rows, y_width), BF16)] + st_shapes,
                [y] + st_bufs, scratch_of(gk), y_width=y_width, state_shapes=state_shapes, slot=g_slot, **gk)
            y = outs[0]
            if grp == "meta":
                meta_states = list(outs[1:])
            else:
                new_bufs[grp] = list(outs[1:])
        return y, new_bufs

    hg_bufs = ret_bufs = ssm_bufs = None
    n_hg, n_ret, n_ssm = state_hgrn.shape[0], state_ret.shape[0], state_ssm.shape[0]
    hg_w_in_b, ret_w_in_b = hg_w_in.astype(BF16), ret_w_in.astype(BF16)
    m_w_in_b = jnp.pad(m_w_in, ((0, 0), (0, 0), (0, M_COLS_PAD - M_COLS))).astype(BF16)
    hg_w_o_b, ret_w_o_b, m_w_o_b = hg_w_o.astype(BF16), ret_w_o.astype(BF16), m_w_o.astype(BF16)
    ln_g4, ln_b4 = ffn[3], ffn[4]
    state_ssm_t = jnp.swapaxes(state_ssm, 3, 4)
    h = None
    y_prompt = y_sample = None
    for i in range(DEPTH):
        if i == 0:
            h = _ffn_ln_first(x_prompt.reshape(n_p, d), x_sample.reshape(n_s, d), meta_tokens.astype(F32), ffn, i, 0, tm_edge)
        else:
            h = _ffn_ln(h, ffn, i, 0, tm)
        kind, j = i % 3, i // 3
        if kind == 0:
            p = _proj(h, hg_w_in_b, j, tm)
            shp = (HG_HEADS, HG_DK, HG_DV)
            y, hg_bufs = run_mixer(
                functools.partial(_hgrn_kernel, layer=i), "hgrn", p, groups(math.gcd(sp, CHUNK), 32),
                lambda grp, gk: [hg_lb_logits, hg_norm_g[j].reshape(1, -1)],
                [state_hgrn], j, HG_HEADS * HG_DV, [shp], n_hg,
                lambda gk: [pltpu.VMEM((gk["nb"],) + shp, F32)], hg_bufs)
            w_o = hg_w_o_b
        elif kind == 1:
            p = _proj(h, ret_w_in_b, j, tm)
            shp = (RET_HEADS, RET_DK, RET_DV)
            grp_cfg = groups(math.gcd(sp, RET_CHUNK), BF16_ROWS)
            rope = {"meta": _rope_tables(jnp.arange(N_META)), "prompt": _rope_tables(N_META + jnp.arange(sp)),
                    "sample": _rope_tables(PAST_LEN + jnp.arange(ss))}
            y, ret_bufs = run_mixer(
                _ret_kernel, "retention", p, grp_cfg,
                lambda grp, gk: [rope[grp][0], rope[grp][1], ret_norm_g[j].reshape(1, -1)],
                [state_ret], j, RET_HEADS * RET_DV, [shp], n_ret,
                lambda gk: [pltpu.VMEM((gk["nb"],) + shp, F32)], ret_bufs)
            w_o = ret_w_o_b
        else:
            p = _proj(h, m_w_in_b, j, tm)
            shp, cshp = (M_HEADS, M_HEADDIM, M_DSTATE), (M_CONV - 1, M_CONV_DIM)
            pad = LANES - M_HEADS
            consts = [m_conv_w[j], m_conv_b[j].reshape(1, -1), jnp.pad(m_dt_bias[j], (0, pad)).reshape(1, LANES),
                      jnp.pad(m_a_log[j], (0, pad)).reshape(1, LANES), jnp.repeat(m_d[j], M_HEADDIM).reshape(1, -1),
                      m_norm_g[j].reshape(1, -1)]
            tables = {}
            y, ssm_bufs = run_mixer(
                _mamba_kernel, "mamba", p, groups(math.gcd(sp, CHUNK), 32),
                lambda grp, gk: consts + list(tables.setdefault(gk["c"], _mamba_tables(gk["c"]))),
                [state_ssm_t, state_conv], j, M_DI, [shp, cshp], n_ssm,
                lambda gk: [pltpu.VMEM((gk["nb"], M_GROUPS, M_GROUP_W, M_DSTATE), F32),
                            pltpu.VMEM((gk["nb"], 8, M_CONV_DIM), F32), pltpu.VMEM((gk["c"], M_CONV_DIM), F32)]
                + ([pltpu.VMEM((gk["nb"] * gk["c"], M_DI), F32)] if gk["nb"] > 1 else []), ssm_bufs)
            w_o = m_w_o_b
        if i < DEPTH - 1:
            h = _out_ln(h, y, w_o, j, ln_g4, ln_b4, i, tm)
            h = _ffn_ln(h, ffn, i, 1, tm)
        else:
            h = _out_ln(h, y, w_o, j, ln_g4, ln_b4, i, tm_edge, n_rows_out=n_p + n_s)
            y_prompt, y_sample = _ffn_ln_last(h, n_p, n_s, ffn, i, 1, tm_edge)

    return (y_prompt.reshape(bp, sp, d), y_sample.reshape(bs, ss, d),
            hg_bufs["prompt"][0], hg_bufs["sample"][0], ret_bufs["prompt"][0], ret_bufs["sample"][0],
            jnp.swapaxes(ssm_bufs["prompt"][0], 3, 4), jnp.swapaxes(ssm_bufs["sample"][0], 3, 4),
            ssm_bufs["prompt"][1], ssm_bufs["sample"][1])
```

```python
import functools
import math

import jax
import jax.numpy as jnp
from jax import lax
from jax.experimental import pallas as pl
from jax.experimental.pallas import tpu as pltpu

F32 = jnp.float32
BF16 = jnp.bfloat16

D_MODEL = 1024
DEPTH = 4
N_META = 16
PAST_LEN = 16384
CHUNK = 64
ALPHA = (2 * DEPTH) ** 0.25
LN_EPS = 1e-5
ROPE_BASE = 10000.0
HG_HEADS, HG_DK, HG_DV = 8, 128, 128
RET_HEADS, RET_DK, RET_DV = 4, 256, 512
RET_CHUNK = 128
HG_BLK = 8
M_DI, M_HEADDIM, M_HEADS, M_GROUPS, M_DSTATE, M_CONV = 2048, 64, 32, 8, 128, 4
M_CONV_DIM = M_DI + 2 * M_GROUPS * M_DSTATE
M_GROUP_W = M_DI // M_GROUPS
M_HPG = M_HEADS // M_GROUPS
M_COLS = M_DI + M_CONV_DIM + M_HEADS
LANES = 128
BF16_ROWS = 16
M_COLS_PAD = ((M_COLS + LANES - 1) // LANES) * LANES
CONV_COLS = 512
VMEM_LIMIT = 56 * 1024 * 1024
MAX_ROW_TILE = 528
LOG2E = 1.4426950408889634


def _params(n_grid, flags=None):
    return pltpu.CompilerParams(dimension_semantics=("arbitrary",) * n_grid, vmem_limit_bytes=VMEM_LIMIT, flags=flags)


def _resident(shape):
    return pl.BlockSpec(shape, lambda *_: (0,) * len(shape), pipeline_mode=pl.Buffered(1))


def _resident_at(shape, lead):
    return pl.BlockSpec((None,) * len(lead) + tuple(shape), lambda *_: tuple(lead) + (0,) * len(shape),
                        pipeline_mode=pl.Buffered(1))


def _dot(a, b):
    return jnp.dot(a.astype(BF16), b.astype(BF16), preferred_element_type=F32)


def _dot_nt(a, b):
    return lax.dot_general(a.astype(BF16), b.astype(BF16), (((1,), (1,)), ((), ())), preferred_element_type=F32)


def _dot_tn(a, b):
    return lax.dot_general(a.astype(BF16), b.astype(BF16), (((0,), (0,)), ((), ())), preferred_element_type=F32)


def _split(x, n_parts):
    parts = []
    r = x
    for _ in range(n_parts - 1):
        hi = r.astype(BF16).astype(F32)
        parts.append(hi)
        r = r - hi
    parts.append(r)
    return parts


def _dot_exact_rhs(a01, x, n_parts=3):
    n = x.shape[1]
    out = jnp.dot(a01, jnp.concatenate(_split(x, n_parts), axis=1).astype(BF16), preferred_element_type=F32)
    res = out[:, :n]
    for i in range(1, n_parts):
        res = res + out[:, i * n:(i + 1) * n]
    return res


def _dot_exact_lhs(xs, b01, n_parts=3):
    r = xs[0].shape[0]
    parts = [p for x in xs for p in _split(x, n_parts)]
    out = jnp.dot(jnp.concatenate(parts, axis=0).astype(BF16), b01, preferred_element_type=F32)
    res = []
    for i in range(len(xs)):
        acc = out[n_parts * i * r:(n_parts * i + 1) * r]
        for k in range(1, n_parts):
            acc = acc + out[(n_parts * i + k) * r:(n_parts * i + k + 1) * r]
        res.append(acc)
    return res


def _cumsum_rows(x, c):
    n = x.shape[0]
    row = lax.broadcasted_iota(jnp.int32, (n, n), 0)
    col = lax.broadcasted_iota(jnp.int32, (n, n), 1)
    tri = (row >= col) if n == c else ((row >= col) & (row // c == col // c))
    return _dot_exact_rhs(jnp.where(tri, 1.0, 0.0).astype(BF16), x)


def _layer_norm(y, g, b):
    mu = jnp.mean(y, axis=-1, keepdims=True)
    d = y - mu
    var = jnp.mean(d * d, axis=-1, keepdims=True)
    return d * lax.rsqrt(var + LN_EPS) * g + b


def _ffn_ln_kernel(*refs, n_src, n_dst, tiles, ff_chunk):
    srcs, (wg_ref, wu_ref, wd_ref, g_ref, b_ref) = refs[:n_src], refs[n_src:n_src + 5]
    dsts = refs[n_src + 5:n_src + 5 + n_dst]
    i = pl.program_id(0)
    if n_src == 1:
        x = srcs[0][...]
    elif n_src == 5:
        h_ref, y_ref, wo_ref, g1_ref, b1_ref = srcs
        mixed = jnp.dot(y_ref[...], wo_ref[...], preferred_element_type=F32)
        x = _layer_norm(ALPHA * h_ref[...] + mixed, g1_ref[...], b1_ref[...])
    else:
        x_scr = refs[-1]
        n_meta = srcs[2].shape[0]

        @pl.when(i < tiles[0])
        def _():
            x_scr[...] = srcs[0][...]

        @pl.when((i >= tiles[0]) & (i < tiles[0] + tiles[1]))
        def _():
            x_scr[...] = srcs[1][...]

        @pl.when(i == tiles[0] + tiles[1])
        def _():
            x_scr[0:n_meta, :] = srcs[2][...]
            if x_scr.shape[0] > n_meta:
                x_scr[n_meta:, :] = jnp.zeros((x_scr.shape[0] - n_meta, x_scr.shape[1]), F32)

        x = x_scr[...]
    xb = x.astype(BF16)
    d_ff = wg_ref.shape[1]
    acc = jnp.zeros(x.shape, F32)
    for lo in range(0, d_ff, ff_chunk):
        gate = jnp.dot(xb, wg_ref[:, lo:lo + ff_chunk], preferred_element_type=F32)
        up = jnp.dot(xb, wu_ref[:, lo:lo + ff_chunk], preferred_element_type=F32)
        act = (jax.nn.silu(gate) * up).astype(BF16)
        acc = acc + jnp.dot(act, wd_ref[lo:lo + ff_chunk, :], preferred_element_type=F32)
    res = _layer_norm(ALPHA * x + 0.5 * acc, g_ref[...], b_ref[...])
    if n_dst == 1:
        dsts[0][...] = res
    else:
        @pl.when(i < tiles[0])
        def _():
            dsts[0][...] = res

        @pl.when(i >= tiles[0])
        def _():
            dsts[1][...] = res


def _ffn_weights(ffn, layer, which):
    wg, wu, wd, ln_g, ln_b = ffn
    d, d_ff = wg.shape[2:]
    ln_row = 0 if which == 0 else 2
    specs = [_resident_at((d, d_ff), (layer, which)), _resident_at((d, d_ff), (layer, which)),
             _resident_at((d_ff, d), (layer, which)), _resident_at((1, d), (layer, ln_row)),
             _resident_at((1, d), (layer, ln_row))]
    return specs, (wg, wu, wd, ln_g, ln_b), d_ff // 11


def _ffn_ln(x, ffn, layer, which, tm):
    m, d = x.shape
    w_specs, w_args, ff_chunk = _ffn_weights(ffn, layer, which)
    return pl.pallas_call(
        functools.partial(_ffn_ln_kernel, n_src=1, n_dst=1, tiles=None, ff_chunk=ff_chunk),
        grid=(m // tm,),
        in_specs=[pl.BlockSpec((tm, d), lambda i: (i, 0))] + w_specs,
        out_specs=pl.BlockSpec((tm, d), lambda i: (i, 0)),
        out_shape=jax.ShapeDtypeStruct((m, d), F32),
        compiler_params=_params(1), name="ffn_ln",
    )(x, *w_args)


def _ffn_ln_first(x_p, x_s, x_m, ffn, layer, which, tm):
    (n_p, d), n_s, n_m = x_p.shape, x_s.shape[0], x_m.shape[0]
    tp, ts = n_p // tm, n_s // tm
    w_specs, w_args, ff_chunk = _ffn_weights(ffn, layer, which)
    return pl.pallas_call(
        functools.partial(_ffn_ln_kernel, n_src=3, n_dst=1, tiles=(tp, ts), ff_chunk=ff_chunk),
        grid=(tp + ts + 1,),
        in_specs=[pl.BlockSpec((tm, d), lambda i: (jnp.minimum(i, tp - 1), 0)),
                  pl.BlockSpec((tm, d), lambda i: (jnp.clip(i - tp, 0, ts - 1), 0)),
                  pl.BlockSpec((n_m, d), lambda i: (0, 0))] + w_specs,
        out_specs=pl.BlockSpec((tm, d), lambda i: (i, 0)),
        out_shape=jax.ShapeDtypeStruct((n_p + n_s + n_m, d), F32),
        scratch_shapes=[pltpu.VMEM((tm, d), F32)],
        compiler_params=_params(1), name="ffn_ln_first",
    )(x_p, x_s, x_m, *w_args)


def _proj_kernel(x_ref, w_ref, o_ref, *, col_chunk):
    xb = x_ref[...].astype(BF16)
    n = w_ref.shape[1]
    for lo in range(0, n, col_chunk):
        hi = min(lo + col_chunk, n)
        o_ref[:, lo:hi] = jnp.dot(xb, w_ref[:, lo:hi], preferred_element_type=F32)


def _proj(x, w, layer, tm):
    m, d = x.shape
    n = w.shape[2]
    return pl.pallas_call(
        functools.partial(_proj_kernel, col_chunk=512),
        grid=(m // tm,),
        in_specs=[pl.BlockSpec((tm, d), lambda i: (i, 0)), _resident_at((d, n), (layer,))],
        out_specs=pl.BlockSpec((tm, n), lambda i: (i, 0)),
        out_shape=jax.ShapeDtypeStruct((m, n), F32),
        compiler_params=_params(1), name="proj",
    )(x, w)


def _mix_ffn_ln(h, y, w_o, w_layer, ffn, layer, tm, split=None):
    m, d = h.shape
    k = y.shape[1]
    ln_g, ln_b = ffn[3], ffn[4]
    w_specs, w_args, ff_chunk = _ffn_weights(ffn, layer, 1)
    mix_specs = [pl.BlockSpec((tm, d), lambda i: (i, 0)), pl.BlockSpec((tm, k), lambda i: (i, 0)),
                 _resident_at((k, d), (w_layer,)), _resident_at((1, d), (layer, 1)), _resident_at((1, d), (layer, 1))]
    if split is None:
        grid, tiles = (m // tm,), None
        out_specs = pl.BlockSpec((tm, d), lambda i: (i, 0))
        out_shape = jax.ShapeDtypeStruct((m, d), F32)
    else:
        tp, ts = split[0] // tm, split[1] // tm
        grid, tiles = (tp + ts,), (tp, ts)
        out_specs = [pl.BlockSpec((tm, d), lambda i: (jnp.minimum(i, tp - 1), 0)),
                     pl.BlockSpec((tm, d), lambda i: (jnp.clip(i - tp, 0, ts - 1), 0))]
        out_shape = [jax.ShapeDtypeStruct((split[0], d), F32), jax.ShapeDtypeStruct((split[1], d), F32)]
    return pl.pallas_call(
        functools.partial(_ffn_ln_kernel, n_src=5, n_dst=1 if split is None else 2, tiles=tiles, ff_chunk=ff_chunk),
        grid=grid, in_specs=mix_specs + w_specs, out_specs=out_specs, out_shape=out_shape,
        compiler_params=_params(1), name="mix_ffn_ln",
    )(h, y, w_o, ln_g, ln_b, *w_args)


def _run_mixer(kernel_fn, name, p, consts, states0, state_layer, out_shapes, out_bufs, scratch, *, y_width,
               state_shapes, row0, bsz, t, c, nb, shared_state, slot):
    n_chunks = t // c
    assert nb == 1 or n_chunks == 1
    rows = nb * c
    assert row0 % rows == 0 and bsz % nb == 0
    blk0 = row0 // rows
    row_idx = lambda sb, ci: (blk0 + sb * n_chunks + ci, 0)
    in_specs = [pl.BlockSpec((rows, p.shape[1]), row_idx)]
    in_specs += [pl.BlockSpec(a.shape, lambda sb, ci, _n=a.ndim: (0,) * _n) for a in consts]
    for shp in state_shapes:
        zeros = (0,) * len(shp)
        in_specs.append(pl.BlockSpec((None, nb) + shp, (lambda sb, ci, _z=zeros: (state_layer, 0) + _z) if shared_state
                                     else (lambda sb, ci, _z=zeros: (state_layer, sb) + _z)))
    out_specs = [pl.BlockSpec((rows, y_width), row_idx)]
    for shp in state_shapes:
        zeros = (0,) * len(shp)
        out_specs.append(pl.BlockSpec((None, nb) + shp, lambda sb, ci, _z=zeros: (slot, sb) + _z))
    args = [p, *consts, *states0]
    n_in = len(args)
    aliases = {}
    for k, buf in enumerate(out_bufs):
        if buf is not None:
            aliases[len(args)] = k
            in_specs.append(pl.BlockSpec(memory_space=pl.ANY))
            args.append(buf)
    return pl.pallas_call(
        functools.partial(kernel_fn, n_alias=len(args) - n_in),
        grid=(bsz // nb, n_chunks), in_specs=in_specs, out_specs=out_specs, out_shape=out_shapes,
        scratch_shapes=scratch, input_output_aliases=aliases,
        compiler_params=_params(2), name=name,
    )(*args)


def _hgrn_kernel(p_ref, lbl_ref, ng_ref, s0_ref, *rest, c, nb, n_chunks, layer, n_alias):
    y_ref, so_ref, s_scr = rest[n_alias:]
    ci = pl.program_id(1)

    @pl.when(ci == 0)
    def _():
        s_scr[...] = s0_ref[...]

    fw = HG_HEADS * HG_DK
    logits = lbl_ref[...]
    e = jnp.exp(logits - jnp.max(logits, axis=0, keepdims=True))
    prob = e / jnp.sum(e, axis=0, keepdims=True)
    lb = prob[0:1] * 0.0
    run = prob[0:1]
    for r in range(1, layer + 1):
        run = run + prob[r:r + 1]
        lb = run - prob[0:1]

    z = p_ref[:, fw:2 * fw]
    ez = jnp.exp(-jnp.abs(z))
    s_big = 1.0 / (1.0 + ez)
    s_small = ez * s_big
    pos = z >= 0.0
    f = lb + (1.0 - lb) * jnp.where(pos, s_big, s_small)
    k_all = (1.0 - lb) * jnp.where(pos, s_small, s_big)
    g_all = _cumsum_rows(jnp.log(f), c)
    g2_all = g_all * LOG2E
    q_all = jax.nn.silu(p_ref[:, 0:fw])
    blk = HG_BLK
    row8 = lax.broadcasted_iota(jnp.int32, (blk, HG_DK), 0)
    own_key = (lax.broadcasted_iota(jnp.int32, (blk * c, c), 0) // blk
               == lax.broadcasted_iota(jnp.int32, (blk * c, c), 1))
    if c > blk:
        n_off = sum(range(blk, c, blk))
        orow = lax.broadcasted_iota(jnp.int32, (c - blk, n_off), 0) // blk
        ocol = lax.broadcasted_iota(jnp.int32, (c - blk, n_off), 1)
        off_mask = jnp.zeros((c - blk, n_off), jnp.bool_)
        start = 0
        for b, r0 in enumerate(range(blk, c, blk)):
            off_mask = off_mask | ((orow == b) & (ocol >= start) & (ocol < start + r0))
            start += r0

    def v_of(h, ro):
        return p_ref[ro:ro + c, 2 * fw + h * HG_DV:2 * fw + (h + 1) * HG_DV]

    staged = {}
    for h in range(HG_HEADS):
        sl = slice(h * HG_DK, (h + 1) * HG_DK)
        for sq in range(nb):
            ro = sq * c
            q, k, g2 = q_all[ro:ro + c, sl], k_all[ro:ro + c, sl], g2_all[ro:ro + c, sl]
            v = v_of(h, ro)
            s = s_scr[sq, h]
            o_inter = _dot(q * jnp.exp2(g2), s)
            pieces = []
            for j in range(c):
                t0 = j // blk * blk
                diff = jnp.where(row8 >= (j - t0), g2[t0:t0 + blk] - g2[j:j + 1], -jnp.inf)
                pieces.append(q[t0:t0 + blk] * jnp.exp2(diff))
            scores = jnp.where(own_key, _dot_nt(jnp.concatenate(pieces, axis=0), k), 0.0)
            att_rows = []
            for t0 in range(0, c, blk):
                acc = scores[t0 * blk:(t0 + 1) * blk]
                for j in range(t0 + 1, t0 + blk):
                    acc = acc + scores[j * blk:(j + 1) * blk]
                att_rows.append(acc)
            att_near = (jnp.concatenate(att_rows, axis=0) if len(att_rows) > 1 else att_rows[0]).astype(BF16)
            att_far = None
            if c > blk:
                q_t = jnp.concatenate([q[r0:r0 + blk] * jnp.exp2(g2[r0:r0 + blk] - g2[r0:r0 + 1])
                                       for r0 in range(blk, c, blk)], axis=0)
                k_t = jnp.concatenate([k[0:r0] * jnp.exp2(g2[r0:r0 + 1] - g2[0:r0]) for r0 in range(blk, c, blk)], axis=0)
                att_far = jnp.where(off_mask, _dot_nt(q_t, k_t), 0.0).astype(BF16)
            g_last = g2[c - 1:c]
            k_dec = k * jnp.exp2(g_last - g2)
            dec_col = jnp.exp2(jnp.broadcast_to(g_last, (HG_DV, HG_DK)).T)
            s_scr[sq, h] = dec_col * s + _dot_tn(k_dec, v)
            staged[h, sq] = (o_inter, att_near, att_far)

    for h in range(HG_HEADS):
        sl = slice(h * HG_DK, (h + 1) * HG_DK)
        y_rows = []
        for sq in range(nb):
            ro = sq * c
            o_inter, att_near, att_far = staged[h, sq]
            v = v_of(h, ro)
            gate = p_ref[ro:ro + c, 2 * fw + HG_HEADS * HG_DV + h * HG_DV:2 * fw + HG_HEADS * HG_DV + (h + 1) * HG_DV]
            o = o_inter + _dot(att_near, v)
            if c > blk:
                v_t = jnp.concatenate([v[0:r0] for r0 in range(blk, c, blk)], axis=0)
                o = o + jnp.concatenate([jnp.zeros((blk, HG_DV), F32), _dot(att_far, v_t)], axis=0)
            ms = jnp.mean(o * o, axis=-1, keepdims=True)
            y_rows.append(o * lax.rsqrt(ms + LN_EPS) * ng_ref[:, sl] * jax.nn.silu(gate))
        y_ref[:, sl] = (jnp.concatenate(y_rows, axis=0) if nb > 1 else y_rows[0]).astype(y_ref.dtype)

    @pl.when(ci == n_chunks - 1)
    def _():
        so_ref[...] = s_scr[...]


def _ret_kernel(p_ref, cos_ref, sin_ref, ng_ref, s0_ref, *rest, c, nb, n_chunks, n_alias):
    y_ref, so_ref, s_scr = rest[n_alias:]
    ci = pl.program_id(1)

    @pl.when(ci == 0)
    def _():
        s_scr[...] = s0_ref[...]

    qk = RET_HEADS * RET_DK
    vw = RET_HEADS * RET_DV
    half = RET_DK // 2
    row = lax.broadcasted_iota(jnp.int32, (c, c), 0)
    col = lax.broadcasted_iota(jnp.int32, (c, c), 1)
    lag = (row - col).astype(F32)
    pos = lax.broadcasted_iota(jnp.int32, (c, 1), 0).astype(F32)
    cos = cos_ref[pl.ds(pl.multiple_of(ci * c, c), c), :]
    sin = sin_ref[pl.ds(pl.multiple_of(ci * c, c), c), :]

    def rot(x):
        x1, x2 = x[:, :half], x[:, half:]
        return jnp.concatenate([x1 * cos - x2 * sin, x1 * sin + x2 * cos], axis=-1)

    def v_of(h, ro):
        return p_ref[ro:ro + c, 2 * qk + h * RET_DV:2 * qk + (h + 1) * RET_DV]

    staged = {}
    for h in range(RET_HEADS):
        log_gamma = math.log(1.0 - 2.0 ** (-5.0 - h))
        dec = jnp.exp(jnp.where(row >= col, lag * log_gamma, -jnp.inf))
        for sq in range(nb):
            ro = sq * c
            q = rot(p_ref[ro:ro + c, h * RET_DK:(h + 1) * RET_DK])
            k = rot(p_ref[ro:ro + c, qk + h * RET_DK:qk + (h + 1) * RET_DK]) * (RET_DK ** -0.5)
            s = s_scr[sq, h]
            att = (_dot_nt(q, k) * dec).astype(BF16)
            o_inter = jnp.exp((pos + 1.0) * log_gamma) * _dot(q, s)
            k_dec = k * jnp.exp((c - 1.0 - pos) * log_gamma)
            s_scr[sq, h] = math.exp(c * log_gamma) * s + _dot_tn(k_dec, v_of(h, ro))
            staged[h, sq] = (att, o_inter)

    for h in range(RET_HEADS):
        sl = slice(h * RET_DV, (h + 1) * RET_DV)
        y_rows = []
        for sq in range(nb):
            ro = sq * c
            att, o_inter = staged[h, sq]
            gate = p_ref[ro:ro + c, 2 * qk + vw + h * RET_DV:2 * qk + vw + (h + 1) * RET_DV]
            o = _dot(att, v_of(h, ro)) + o_inter

            mu = jnp.mean(o, axis=-1, keepdims=True)
            d = o - mu
            var = jnp.mean(d * d, axis=-1, keepdims=True)
            y_rows.append(d * lax.rsqrt(var + LN_EPS) * ng_ref[:, sl] * jax.nn.silu(gate))
        y_ref[:, sl] = (jnp.concatenate(y_rows, axis=0) if nb > 1 else y_rows[0]).astype(y_ref.dtype)

    @pl.when(ci == n_chunks - 1)
    def _():
        so_ref[...] = s_scr[...]


def _mamba_stack_width(c):
    return max(M_HPG * c, LANES)


def _mamba_tables(c):
    w = _mamba_stack_width(c)
    head = jnp.arange(LANES)[:, None]
    ch = jnp.arange(M_DI)[None, :]
    expand_ch = (ch // M_HEADDIM == head).astype(BF16)
    q = jnp.arange(M_GROUPS * w)[None, :]
    r = q % w
    valid = r < M_HPG * c
    expand_k = ((head == (q // w) * M_HPG + r // c) & valid).astype(BF16)
    i = jnp.arange(c)[:, None]
    key = r % c
    neg = jnp.where((i >= key) & valid, 0.0, -jnp.inf).astype(F32)
    diag = ((i == key) & valid).astype(F32)
    return expand_ch, expand_ch.T, expand_k, neg, diag


def _mamba_kernel(p_ref, cw_ref, cb_ref, dtb_ref, alog_ref, dskip_ref, ng_ref, ech_ref, echt_ref, ek_ref, neg_ref,
                  diag_ref, s0_ref, c0_ref, *rest, c, nb, n_chunks, n_alias):
    y_ref, so_ref, co_ref, s_scr, x_scr, xbc_scr = rest[n_alias:n_alias + 6]
    y_scr = rest[n_alias + 6] if nb > 1 else None
    first_row = lax.broadcasted_iota(jnp.int32, (8, 1), 0) == 0
    ci = pl.program_id(1)
    keep = M_CONV - 1
    gn = M_GROUPS * M_DSTATE
    w = _mamba_stack_width(c)
    used = M_HPG * c

    @pl.when(ci == 0)
    def _():
        for sq in range(nb):
            for grp in range(M_GROUPS):
                s_scr[sq, grp] = s0_ref[sq, grp * M_HPG:(grp + 1) * M_HPG].reshape(M_GROUP_W, M_DSTATE)
            r0, r1, r2 = c0_ref[sq, 0:1], c0_ref[sq, 1:2], c0_ref[sq, 2:3]
            x_scr[sq, 0:1, :] = cw_ref[0:1, :] * r2
            x_scr[sq, 1:2, :] = cw_ref[1:2, :] * r2 + cw_ref[0:1, :] * r1
            x_scr[sq, 2:3, :] = cw_ref[2:3, :] * r2 + cw_ref[1:2, :] * r1 + cw_ref[0:1, :] * r0

    dt = jax.nn.softplus(p_ref[:, M_DI + M_CONV_DIM:M_DI + M_CONV_DIM + LANES] + dtb_ref[...])
    g_all = _cumsum_rows(dt * -jnp.exp(alog_ref[...]), c)
    gk_all, = _dot_exact_lhs([g_all], ek_ref[...])
    blockmask = (lax.broadcasted_iota(jnp.int32, (w, M_GROUP_W), 1) // M_HEADDIM
                 == lax.broadcasted_iota(jnp.int32, (w, M_GROUP_W), 0) // c)

    for sq in range(nb):
        ro = sq * c
        for lo in range(0, M_CONV_DIM, CONV_COLS):
            cs = slice(lo, lo + CONV_COLS)
            x = p_ref[ro:ro + c, M_DI + lo:M_DI + lo + CONV_COLS]
            u = x * cw_ref[0:1, cs]
            for wi in range(1, M_CONV):
                carry = x_scr[sq, wi - 1:wi, cs]
                x_scr[sq, wi - 1:wi, cs] = u[c - 1:c]
                rolled = pltpu.roll(u, 1, 0)
                head_rows = jnp.where(first_row, carry, rolled[0:8])
                shifted = jnp.concatenate([head_rows, rolled[8:]], axis=0) if c > 8 else head_rows
                u = x * cw_ref[wi:wi + 1, cs] + shifted
            xbc_scr[:, cs] = jax.nn.silu(u + cb_ref[:, cs])

        @pl.when(ci == n_chunks - 1)
        def _():
            co_ref[sq] = p_ref[ro + c - keep:ro + c, M_DI:M_DI + M_CONV_DIM]

        g = g_all[ro:ro + c]
        g_last = g[c - 1:c]
        dt_x, eg_x, kf_x = _dot_exact_lhs([dt[ro:ro + c], jnp.exp(g), jnp.exp(g_last - g)], ech_ref[...], n_parts=2)
        egl_rows = jnp.exp(jnp.broadcast_to(g_last, (LANES, LANES)).T)
        egl_col = _dot_exact_rhs(echt_ref[...], egl_rows, n_parts=2)
        gk = gk_all[ro:ro + c]
        g_row = jnp.sum(gk * diag_ref[...], axis=0, keepdims=True)
        dec_all = jnp.exp(gk - g_row + neg_ref[...])

        atts = []
        for grp in range(M_GROUPS):
            bmat = xbc_scr[:, M_DI + grp * M_DSTATE:M_DI + (grp + 1) * M_DSTATE]
            cmat = xbc_scr[:, M_DI + gn + grp * M_DSTATE:M_DI + gn + (grp + 1) * M_DSTATE]
            pad_rows = [jnp.zeros((w - used, M_DSTATE), F32)] if w > used else []
            b_cat = jnp.concatenate([bmat] * M_HPG + pad_rows, axis=0)
            atts.append((_dot_nt(cmat, b_cat) * dec_all[:, grp * w:(grp + 1) * w]).astype(BF16))

        for grp in range(M_GROUPS):
            gsl = slice(grp * M_GROUP_W, (grp + 1) * M_GROUP_W)
            xs = xbc_scr[:, gsl]
            bmat = xbc_scr[:, M_DI + grp * M_DSTATE:M_DI + (grp + 1) * M_DSTATE]
            cmat = xbc_scr[:, M_DI + gn + grp * M_DSTATE:M_DI + gn + (grp + 1) * M_DSTATE]
            xdt = xs * dt_x[:, gsl]
            s_t = s_scr[sq, grp]
            pad_rows = [jnp.zeros((w - used, M_GROUP_W), F32)] if w > used else []
            x_cat = jnp.where(blockmask, jnp.concatenate([xdt] * M_HPG + pad_rows, axis=0), 0.0)
            o = _dot(atts[grp], x_cat) + eg_x[:, gsl] * _dot_nt(cmat, s_t)
            s_scr[sq, grp] = egl_col[gsl] * s_t + _dot_tn(xdt * kf_x[:, gsl], bmat)

            y = (o + dskip_ref[:, gsl] * xs) * jax.nn.silu(p_ref[ro:ro + c, gsl])
            ms = jnp.mean(y * y, axis=-1, keepdims=True)
            y = y * lax.rsqrt(ms + LN_EPS) * ng_ref[:, gsl]
            if nb > 1:
                y_scr[ro:ro + c, gsl] = y
            else:
                y_ref[:, gsl] = y.astype(y_ref.dtype)

    if nb > 1:
        y_ref[...] = y_scr[...].astype(y_ref.dtype)

    @pl.when(ci == n_chunks - 1)
    def _():
        for sq in range(nb):
            for grp in range(M_GROUPS):
                so_ref[sq, grp * M_HPG:(grp + 1) * M_HPG] = s_scr[sq, grp].reshape(M_HPG, M_HEADDIM, M_DSTATE)


def _rope_tables(pos):
    half = RET_DK // 2
    inv_freq = ROPE_BASE ** (-jnp.arange(half, dtype=F32) / half)
    ang = pos.astype(F32)[:, None] * inv_freq[None, :]
    return jnp.cos(ang), jnp.sin(ang)


def _row_tile(n_rows):
    for cand in range(min(MAX_ROW_TILE, n_rows) // BF16_ROWS * BF16_ROWS, 0, -BF16_ROWS):
        if n_rows % cand == 0:
            return cand
    raise ValueError(f"no row tile for {n_rows} rows")


def kernel(x_prompt, x_sample, state_hgrn, state_ret, state_ssm, state_conv, meta_tokens, ln_g, ln_b,
           ffn_w_gate, ffn_w_up, ffn_w_down, hg_lb_logits, hg_w_in, hg_norm_g, hg_w_o,
           ret_w_in, ret_norm_g, ret_w_o, m_w_in, m_conv_w, m_conv_b, m_dt_bias, m_a_log, m_d,
           m_norm_g, m_w_o):
    bp, sp, d = x_prompt.shape
    bs, ss, _ = x_sample.shape
    n_p, n_s = bp * sp, bs * ss
    n_rows = n_p + n_s + N_META
    tm = _row_tile(n_rows)
    tm_edge = math.gcd(math.gcd(n_p, n_s), 512)
    c_s = math.gcd(ss, CHUNK)
    c_m = math.gcd(N_META, CHUNK)

    def groups(c_prompt, sample_rows):
        nb_s = 1
        if ss == c_s:
            nb_s = max(1, math.gcd(bs, max(sample_rows, BF16_ROWS) // c_s))
        return {"meta": dict(row0=n_p + n_s, bsz=1, t=N_META, c=c_m, nb=1, shared_state=True),
                "prompt": dict(row0=0, bsz=bp, t=sp, c=c_prompt, nb=1, shared_state=True),
                "sample": dict(row0=n_p, bsz=bs, t=ss, c=c_s, nb=nb_s, shared_state=False)}

    bsz_of = {"prompt": bp, "sample": bs}
    ffn = (ffn_w_gate.astype(BF16), ffn_w_up.astype(BF16), ffn_w_down.astype(BF16),
           ln_g.reshape(DEPTH, 3, 1, d), ln_b.reshape(DEPTH, 3, 1, d))

    def run_mixer(kernel_fn, name, p, grp_cfg, consts_of, states_in, layer, y_width, state_shapes, n_slots,
                  scratch_of, bufs):
        y = None
        meta_states = None
        new_bufs = {}
        for grp in ("meta", "prompt", "sample"):
            gk = grp_cfg[grp]
            if grp == "meta":
                states0 = [jnp.zeros((1, 1) + shp, F32) for shp in state_shapes]
                st_shapes = [jax.ShapeDtypeStruct((1, 1) + shp, F32) for shp in state_shapes]
                st_bufs, st_layer, g_slot = [None] * len(state_shapes), 0, 0
            else:
                states0, st_layer = (meta_states, 0) if grp == "prompt" else (states_in, layer)
                st_shapes = [jax.ShapeDtypeStruct((n_slots, bsz_of[grp]) + shp, F32) for shp in state_shapes]
                st_bufs = bufs[grp] if bufs is not None else [None] * len(state_shapes)
                g_slot = layer
            outs = _run_mixer(
                functools.partial(kernel_fn, c=gk["c"], nb=gk["nb"], n_chunks=gk["t"] // gk["c"]), name + "_" + grp,
                p, consts_of(grp, gk), states0, st_layer, [jax.ShapeDtypeStruct((n_rows, y_width), BF16)] + st_shapes,
                [y] + st_bufs, scratch_of(gk), y_width=y_width, state_shapes=state_shapes, slot=g_slot, **gk)
            y = outs[0]
            if grp == "meta":
                meta_states = list(outs[1:])
            else:
                new_bufs[grp] = list(outs[1:])
        return y, new_bufs

    hg_bufs = ret_bufs = ssm_bufs = None
    n_hg, n_ret, n_ssm = state_hgrn.shape[0], state_ret.shape[0], state_ssm.shape[0]
    hg_w_in_b, ret_w_in_b = hg_w_in.astype(BF16), ret_w_in.astype(BF16)
    m_w_in_b = jnp.pad(m_w_in, ((0, 0), (0, 0), (0, M_COLS_PAD - M_COLS))).astype(BF16)
    hg_w_o_b, ret_w_o_b, m_w_o_b = hg_w_o.astype(BF16), ret_w_o.astype(BF16), m_w_o.astype(BF16)
    state_ssm_t = jnp.swapaxes(state_ssm, 3, 4)
    h = None
    y_prompt = y_sample = None
    for i in range(DEPTH):
        if i == 0:
            h = _ffn_ln_first(x_prompt.reshape(n_p, d), x_sample.reshape(n_s, d), meta_tokens.astype(F32), ffn, i, 0, tm_edge)
        else:
            h = _ffn_ln(h, ffn, i, 0, tm)
        kind, j = i % 3, i // 3
        if kind == 0:
            p = _proj(h, hg_w_in_b, j, tm)
            shp = (HG_HEADS, HG_DK, HG_DV)
            y, hg_bufs = run_mixer(
                functools.partial(_hgrn_kernel, layer=i), "hgrn", p, groups(math.gcd(sp, CHUNK), 32),
                lambda grp, gk: [hg_lb_logits, hg_norm_g[j].reshape(1, -1)],
                [state_hgrn], j, HG_HEADS * HG_DV, [shp], n_hg,
                lambda gk: [pltpu.VMEM((gk["nb"],) + shp, F32)], hg_bufs)
            w_o = hg_w_o_b
        elif kind == 1:
            p = _proj(h, ret_w_in_b, j, tm)
            shp = (RET_HEADS, RET_DK, RET_DV)
            grp_cfg = groups(math.gcd(sp, RET_CHUNK), BF16_ROWS)
            rope = {"meta": _rope_tables(jnp.arange(N_META)), "prompt": _rope_tables(N_META + jnp.arange(sp)),
                    "sample": _rope_tables(PAST_LEN + jnp.arange(ss))}
            y, ret_bufs = run_mixer(
                _ret_kernel, "retention", p, grp_cfg,
                lambda grp, gk: [rope[grp][0], rope[grp][1], ret_norm_g[j].reshape(1, -1)],
                [state_ret], j, RET_HEADS * RET_DV, [shp], n_ret,
                lambda gk: [pltpu.VMEM((gk["nb"],) + shp, F32)], ret_bufs)
            w_o = ret_w_o_b
        else:
            p = _proj(h, m_w_in_b, j, tm)
            shp, cshp = (M_HEADS, M_HEADDIM, M_DSTATE), (M_CONV - 1, M_CONV_DIM)
            pad = LANES - M_HEADS
            consts = [m_conv_w[j], m_conv_b[j].reshape(1, -1), jnp.pad(m_dt_bias[j], (0, pad)).reshape(1, LANES),
                      jnp.pad(m_a_log[j], (0, pad)).reshape(1, LANES), jnp.repeat(m_d[j], M_HEADDIM).reshape(1, -1),
                      m_norm_g[j].reshape(1, -1)]
            tables = {}
            y, ssm_bufs = run_mixer(
                _mamba_kernel, "mamba", p, groups(math.gcd(sp, CHUNK), 32),
                lambda grp, gk: consts + list(tables.setdefault(gk["c"], _mamba_tables(gk["c"]))),
                [state_ssm_t, state_conv], j, M_DI, [shp, cshp], n_ssm,
                lambda gk: [pltpu.VMEM((gk["nb"], M_GROUPS, M_GROUP_W, M_DSTATE), F32),
                            pltpu.VMEM((gk["nb"], 8, M_CONV_DIM), F32), pltpu.VMEM((gk["c"], M_CONV_DIM), F32)]
                + ([pltpu.VMEM((gk["nb"] * gk["c"], M_DI), F32)] if gk["nb"] > 1 else []), ssm_bufs)
            w_o = m_w_o_b
        if i < DEPTH - 1:
            h = _mix_ffn_ln(h, y, w_o, j, ffn, i, tm)
        else:
            y_prompt, y_sample = _mix_ffn_ln(h, y, w_o, j, ffn, i, tm_edge, split=(n_p, n_s))

    return (y_prompt.reshape(bp, sp, d), y_sample.reshape(bs, ss, d),
            hg_bufs["prompt"][0], hg_bufs["sample"][0], ret_bufs["prompt"][0], ret_bufs["sample"][0],
            jnp.swapaxes(ssm_bufs["prompt"][0], 3, 4), jnp.swapaxes(ssm_bufs["sample"][0], 3, 4),
            ssm_bufs["prompt"][1], ssm_bufs["sample"][1])
```

```python
import functools
import math

import jax
import jax.numpy as jnp
from jax import lax
from jax.experimental import pallas as pl
from jax.experimental.pallas import tpu as pltpu

F32 = jnp.float32
BF16 = jnp.bfloat16

D_MODEL = 1024
DEPTH = 4
N_META = 16
PAST_LEN = 16384
CHUNK = 64
ALPHA = (2 * DEPTH) ** 0.25
LN_EPS = 1e-5
ROPE_BASE = 10000.0
HG_HEADS, HG_DK, HG_DV = 8, 128, 128
RET_HEADS, RET_DK, RET_DV = 4, 256, 512
RET_CHUNK = 128
HG_BLK = 8
M_DI, M_HEADDIM, M_HEADS, M_GROUPS, M_DSTATE, M_CONV = 2048, 64, 32, 8, 128, 4
M_CONV_DIM = M_DI + 2 * M_GROUPS * M_DSTATE
M_GROUP_W = M_DI // M_GROUPS
M_HPG = M_HEADS // M_GROUPS
M_COLS = M_DI + M_CONV_DIM + M_HEADS
LANES = 128
BF16_ROWS = 16
M_COLS_PAD = ((M_COLS + LANES - 1) // LANES) * LANES
CONV_COLS = 512
VMEM_LIMIT = 56 * 1024 * 1024
MAX_ROW_TILE = 528
LOG2E = 1.4426950408889634


def _params(n_grid, flags=None):
    return pltpu.CompilerParams(dimension_semantics=("arbitrary",) * n_grid, vmem_limit_bytes=VMEM_LIMIT, flags=flags)


def _resident(shape):
    return pl.BlockSpec(shape, lambda *_: (0,) * len(shape), pipeline_mode=pl.Buffered(1))


def _resident_at(shape, lead):
    return pl.BlockSpec((None,) * len(lead) + tuple(shape), lambda *_: tuple(lead) + (0,) * len(shape),
                        pipeline_mode=pl.Buffered(1))


def _dot(a, b):
    return jnp.dot(a.astype(BF16), b.astype(BF16), preferred_element_type=F32)


def _dot_nt(a, b):
    return lax.dot_general(a.astype(BF16), b.astype(BF16), (((1,), (1,)), ((), ())), preferred_element_type=F32)


def _dot_tn(a, b):
    return lax.dot_general(a.astype(BF16), b.astype(BF16), (((0,), (0,)), ((), ())), preferred_element_type=F32)


def _split(x, n_parts):
    parts = []
    r = x
    for _ in range(n_parts - 1):
        hi = r.astype(BF16).astype(F32)
        parts.append(hi)
        r = r - hi
    parts.append(r)
    return parts


def _dot_exact_rhs(a01, x, n_parts=3):
    n = x.shape[1]
    out = jnp.dot(a01, jnp.concatenate(_split(x, n_parts), axis=1).astype(BF16), preferred_element_type=F32)
    res = out[:, :n]
    for i in range(1, n_parts):
        res = res + out[:, i * n:(i + 1) * n]
    return res


def _dot_exact_lhs(xs, b01, n_parts=3):
    r = xs[0].shape[0]
    parts = [p for x in xs for p in _split(x, n_parts)]
    out = jnp.dot(jnp.concatenate(parts, axis=0).astype(BF16), b01, preferred_element_type=F32)
    res = []
    for i in range(len(xs)):
        acc = out[n_parts * i * r:(n_parts * i + 1) * r]
        for k in range(1, n_parts):
            acc = acc + out[(n_parts * i + k) * r:(n_parts * i + k + 1) * r]
        res.append(acc)
    return res


def _cumsum_rows(x, c):
    n = x.shape[0]
    row = lax.broadcasted_iota(jnp.int32, (n, n), 0)
    col = lax.broadcasted_iota(jnp.int32, (n, n), 1)
    tri = (row >= col) if n == c else ((row >= col) & (row // c == col // c))
    return _dot_exact_rhs(jnp.where(tri, 1.0, 0.0).astype(BF16), x)


def _layer_norm(y, g, b):
    mu = jnp.mean(y, axis=-1, keepdims=True)
    d = y - mu
    var = jnp.mean(d * d, axis=-1, keepdims=True)
    return d * lax.rsqrt(var + LN_EPS) * g + b


def _ffn_ln_kernel(*refs, n_src, n_dst, tiles, ff_chunk):
    srcs, (wg_ref, wu_ref, wd_ref, g_ref, b_ref) = refs[:n_src], refs[n_src:n_src + 5]
    dsts = refs[n_src + 5:n_src + 5 + n_dst]
    i = pl.program_id(0)
    if n_src == 1:
        x = srcs[0][...]
    elif n_src == 5:
        h_ref, y_ref, wo_ref, g1_ref, b1_ref = srcs
        mixed = jnp.dot(y_ref[...], wo_ref[...], preferred_element_type=F32)
        x = _layer_norm(ALPHA * h_ref[...] + mixed, g1_ref[...], b1_ref[...])
    else:
        x_scr = refs[-1]
        n_meta = srcs[2].shape[0]

        @pl.when(i < tiles[0])
        def _():
            x_scr[...] = srcs[0][...]

        @pl.when((i >= tiles[0]) & (i < tiles[0] + tiles[1]))
        def _():
            x_scr[...] = srcs[1][...]

        @pl.when(i == tiles[0] + tiles[1])
        def _():
            x_scr[0:n_meta, :] = srcs[2][...]
            if x_scr.shape[0] > n_meta:
                x_scr[n_meta:, :] = jnp.zeros((x_scr.shape[0] - n_meta, x_scr.shape[1]), F32)

        x = x_scr[...]
    xb = x.astype(BF16)
    d_ff = wg_ref.shape[1]
    acc = jnp.zeros(x.shape, F32)
    for lo in range(0, d_ff, ff_chunk):
        gate = jnp.dot(xb, wg_ref[:, lo:lo + ff_chunk].astype(BF16), preferred_element_type=F32)
        up = jnp.dot(xb, wu_ref[:, lo:lo + ff_chunk].astype(BF16), preferred_element_type=F32)
        act = (jax.nn.silu(gate) * up).astype(BF16)
        acc = acc + jnp.dot(act, wd_ref[lo:lo + ff_chunk, :].astype(BF16), preferred_element_type=F32)
    res = _layer_norm(ALPHA * x + 0.5 * acc, g_ref[...], b_ref[...])
    if n_dst == 1:
        dsts[0][...] = res
    else:
        @pl.when(i < tiles[0])
        def _():
            dsts[0][...] = res

        @pl.when(i >= tiles[0])
        def _():
            dsts[1][...] = res


def _ffn_weights(ffn, layer, which):
    wg, wu, wd, ln_g, ln_b = ffn
    d, d_ff = wg.shape[2:]
    ln_row = 0 if which == 0 else 2
    w_idx = (layer, which if wg.shape[1] > 1 else 0)
    specs = [_resident_at((d, d_ff), w_idx), _resident_at((d, d_ff), w_idx),
             _resident_at((d_ff, d), w_idx), _resident_at((1, d), (layer, ln_row)),
             _resident_at((1, d), (layer, ln_row))]
    return specs, (wg, wu, wd, ln_g, ln_b), d_ff // 11


def _ffn_ln(x, ffn, layer, which, tm):
    m, d = x.shape
    w_specs, w_args, ff_chunk = _ffn_weights(ffn, layer, which)
    return pl.pallas_call(
        functools.partial(_ffn_ln_kernel, n_src=1, n_dst=1, tiles=None, ff_chunk=ff_chunk),
        grid=(m // tm,),
        in_specs=[pl.BlockSpec((tm, d), lambda i: (i, 0))] + w_specs,
        out_specs=pl.BlockSpec((tm, d), lambda i: (i, 0)),
        out_shape=jax.ShapeDtypeStruct((m, d), F32),
        compiler_params=_params(1), name="ffn_ln",
    )(x, *w_args)


def _ffn_ln_first(x_p, x_s, x_m, ffn, layer, which, tm):
    (n_p, d), n_s, n_m = x_p.shape, x_s.shape[0], x_m.shape[0]
    tp, ts = n_p // tm, n_s // tm
    w_specs, w_args, ff_chunk = _ffn_weights(ffn, layer, which)
    return pl.pallas_call(
        functools.partial(_ffn_ln_kernel, n_src=3, n_dst=1, tiles=(tp, ts), ff_chunk=ff_chunk),
        grid=(tp + ts + 1,),
        in_specs=[pl.BlockSpec((tm, d), lambda i: (jnp.minimum(i, tp - 1), 0)),
                  pl.BlockSpec((tm, d), lambda i: (jnp.clip(i - tp, 0, ts - 1), 0)),
                  pl.BlockSpec((n_m, d), lambda i: (0, 0))] + w_specs,
        out_specs=pl.BlockSpec((tm, d), lambda i: (i, 0)),
        out_shape=jax.ShapeDtypeStruct((n_p + n_s + n_m, d), F32),
        scratch_shapes=[pltpu.VMEM((tm, d), F32)],
        compiler_params=_params(1), name="ffn_ln_first",
    )(x_p, x_s, x_m, *w_args)


def _proj_kernel(x_ref, w_ref, o_ref, *, col_chunk):
    xb = x_ref[...].astype(BF16)
    n = w_ref.shape[1]
    for lo in range(0, n, col_chunk):
        hi = min(lo + col_chunk, n)
        o_ref[:, lo:hi] = jnp.dot(xb, w_ref[:, lo:hi], preferred_element_type=F32)


def _proj(x, w, layer, tm):
    m, d = x.shape
    n = w.shape[2]
    return pl.pallas_call(
        functools.partial(_proj_kernel, col_chunk=512),
        grid=(m // tm,),
        in_specs=[pl.BlockSpec((tm, d), lambda i: (i, 0)), _resident_at((d, n), (layer,))],
        out_specs=pl.BlockSpec((tm, n), lambda i: (i, 0)),
        out_shape=jax.ShapeDtypeStruct((m, n), F32),
        compiler_params=_params(1), name="proj",
    )(x, w)


def _mix_ffn_ln(h, y, w_o, w_layer, ffn, layer, tm, split=None):
    m, d = h.shape
    k = y.shape[1]
    ln_g, ln_b = ffn[3], ffn[4]
    w_specs, w_args, ff_chunk = _ffn_weights(ffn, layer, 1)
    mix_specs = [pl.BlockSpec((tm, d), lambda i: (i, 0)), pl.BlockSpec((tm, k), lambda i: (i, 0)),
                 _resident_at((k, d), (w_layer,)), _resident_at((1, d), (layer, 1)), _resident_at((1, d), (layer, 1))]
    if split is None:
        grid, tiles = (m // tm,), None
        out_specs = pl.BlockSpec((tm, d), lambda i: (i, 0))
        out_shape = jax.ShapeDtypeStruct((m, d), F32)
    else:
        tp, ts = split[0] // tm, split[1] // tm
        grid, tiles = (tp + ts,), (tp, ts)
        out_specs = [pl.BlockSpec((tm, d), lambda i: (jnp.minimum(i, tp - 1), 0)),
                     pl.BlockSpec((tm, d), lambda i: (jnp.clip(i - tp, 0, ts - 1), 0))]
        out_shape = [jax.ShapeDtypeStruct((split[0], d), F32), jax.ShapeDtypeStruct((split[1], d), F32)]
    return pl.pallas_call(
        functools.partial(_ffn_ln_kernel, n_src=5, n_dst=1 if split is None else 2, tiles=tiles, ff_chunk=ff_chunk),
        grid=grid, in_specs=mix_specs + w_specs, out_specs=out_specs, out_shape=out_shape,
        compiler_params=_params(1), name="mix_ffn_ln",
    )(h, y, w_o, ln_g, ln_b, *w_args)


def _run_mixer(kernel_fn, name, p, consts, states0, state_layer, out_shapes, out_bufs, scratch, *, y_width,
               state_shapes, row0, bsz, t, c, nb, shared_state, slot):
    n_chunks = t // c
    assert nb == 1 or n_chunks == 1
    rows = nb * c
    assert row0 % rows == 0 and bsz % nb == 0
    blk0 = row0 // rows
    row_idx = lambda sb, ci: (blk0 + sb * n_chunks + ci, 0)
    in_specs = [pl.BlockSpec((rows, p.shape[1]), row_idx)]
    in_specs += [pl.BlockSpec(a.shape, lambda sb, ci, _n=a.ndim: (0,) * _n) for a in consts]
    for shp in state_shapes:
        zeros = (0,) * len(shp)
        in_specs.append(pl.BlockSpec((None, nb) + shp, (lambda sb, ci, _z=zeros: (state_layer, 0) + _z) if shared_state
                                     else (lambda sb, ci, _z=zeros: (state_layer, sb) + _z)))
    out_specs = [pl.BlockSpec((rows, y_width), row_idx)]
    for shp in state_shapes:
        zeros = (0,) * len(shp)
        out_specs.append(pl.BlockSpec((None, nb) + shp, lambda sb, ci, _z=zeros: (slot, sb) + _z))
    args = [p, *consts, *states0]
    n_in = len(args)
    aliases = {}
    for k, buf in enumerate(out_bufs):
        if buf is not None:
            aliases[len(args)] = k
            in_specs.append(pl.BlockSpec(memory_space=pl.ANY))
            args.append(buf)
    return pl.pallas_call(
        functools.partial(kernel_fn, n_alias=len(args) - n_in),
        grid=(bsz // nb, n_chunks), in_specs=in_specs, out_specs=out_specs, out_shape=out_shapes,
        scratch_shapes=scratch, input_output_aliases=aliases,
        compiler_params=_params(2), name=name,
    )(*args)


def _hgrn_kernel(p_ref, lbl_ref, ng_ref, s0_ref, *rest, c, nb, n_chunks, layer, n_alias):
    y_ref, so_ref, s_scr = rest[n_alias:]
    ci = pl.program_id(1)

    @pl.when(ci == 0)
    def _():
        s_scr[...] = s0_ref[...]

    fw = HG_HEADS * HG_DK
    logits = lbl_ref[...]
    e = jnp.exp(logits - jnp.max(logits, axis=0, keepdims=True))
    prob = e / jnp.sum(e, axis=0, keepdims=True)
    lb = prob[0:1] * 0.0
    run = prob[0:1]
    for r in range(1, layer + 1):
        run = run + prob[r:r + 1]
        lb = run - prob[0:1]

    z = p_ref[:, fw:2 * fw]
    ez = jnp.exp(-jnp.abs(z))
    s_big = 1.0 / (1.0 + ez)
    s_small = ez * s_big
    pos = z >= 0.0
    f = lb + (1.0 - lb) * jnp.where(pos, s_big, s_small)
    k_all = (1.0 - lb) * jnp.where(pos, s_small, s_big)
    g_all = _cumsum_rows(jnp.log(f), c)
    g2_all = g_all * LOG2E
    q_all = jax.nn.silu(p_ref[:, 0:fw])
    blk = HG_BLK
    row8 = lax.broadcasted_iota(jnp.int32, (blk, HG_DK), 0)
    causal = [jnp.where(row8 >= jj, 0.0, -jnp.inf) for jj in range(blk)]
    own_key = (lax.broadcasted_iota(jnp.int32, (blk * c, c), 0) // blk
               == lax.broadcasted_iota(jnp.int32, (blk * c, c), 1))
    if c > blk:
        n_off = sum(range(blk, c, blk))
        orow = lax.broadcasted_iota(jnp.int32, (c - blk, n_off), 0) // blk
        ocol = lax.broadcasted_iota(jnp.int32, (c - blk, n_off), 1)
        off_mask = jnp.zeros((c - blk, n_off), jnp.bool_)
        start = 0
        for b, r0 in enumerate(range(blk, c, blk)):
            off_mask = off_mask | ((orow == b) & (ocol >= start) & (ocol < start + r0))
            start += r0

    def v_of(h, ro):
        return p_ref[ro:ro + c, 2 * fw + h * HG_DV:2 * fw + (h + 1) * HG_DV]

    staged = {}
    for h in range(HG_HEADS):
        sl = slice(h * HG_DK, (h + 1) * HG_DK)
        for sq in range(nb):
            ro = sq * c
            q, k, g2 = q_all[ro:ro + c, sl], k_all[ro:ro + c, sl], g2_all[ro:ro + c, sl]
            v = v_of(h, ro)
            s = s_scr[sq, h]
            o_inter = _dot(q * jnp.exp2(g2), s)
            pieces = []
            for j in range(c):
                t0 = j // blk * blk
                diff = g2[t0:t0 + blk] - g2[j:j + 1]
                if j > t0:
                    diff = diff + causal[j - t0]
                pieces.append(q[t0:t0 + blk] * jnp.exp2(diff))
            scores = jnp.where(own_key, _dot_nt(jnp.concatenate(pieces, axis=0), k), 0.0)
            att_rows = []
            for t0 in range(0, c, blk):
                acc = scores[t0 * blk:(t0 + 1) * blk]
                for j in range(t0 + 1, t0 + blk):
                    acc = acc + scores[j * blk:(j + 1) * blk]
                att_rows.append(acc)
            att_near = (jnp.concatenate(att_rows, axis=0) if len(att_rows) > 1 else att_rows[0]).astype(BF16)
            att_far = None
            if c > blk:
                q_t = jnp.concatenate([q[r0:r0 + blk] * jnp.exp2(g2[r0:r0 + blk] - g2[r0:r0 + 1])
                                       for r0 in range(blk, c, blk)], axis=0)
                k_t = jnp.concatenate([k[0:r0] * jnp.exp2(g2[r0:r0 + 1] - g2[0:r0]) for r0 in range(blk, c, blk)], axis=0)
                att_far = jnp.where(off_mask, _dot_nt(q_t, k_t), 0.0).astype(BF16)
            g_last = g2[c - 1:c]
            k_dec = k * jnp.exp2(g_last - g2)
            dec_col = jnp.exp2(jnp.broadcast_to(g_last, (HG_DV, HG_DK)).T)
            s_scr[sq, h] = dec_col * s + _dot_tn(k_dec, v)
            staged[h, sq] = (o_inter, att_near, att_far)

    for h in range(HG_HEADS):
        sl = slice(h * HG_DK, (h + 1) * HG_DK)
        y_rows = []
        for sq in range(nb):
            ro = sq * c
            o_inter, att_near, att_far = staged[h, sq]
            v = v_of(h, ro)
            gate = p_ref[ro:ro + c, 2 * fw + HG_HEADS * HG_DV + h * HG_DV:2 * fw + HG_HEADS * HG_DV + (h + 1) * HG_DV]
            o = o_inter + _dot(att_near, v)
            if c > blk:
                v_t = jnp.concatenate([v[0:r0] for r0 in range(blk, c, blk)], axis=0)
                o = o + jnp.concatenate([jnp.zeros((blk, HG_DV), F32), _dot(att_far, v_t)], axis=0)
            ms = jnp.mean(o * o, axis=-1, keepdims=True)
            y_rows.append(o * lax.rsqrt(ms + LN_EPS) * ng_ref[:, sl] * jax.nn.silu(gate))
        y_ref[:, sl] = (jnp.concatenate(y_rows, axis=0) if nb > 1 else y_rows[0]).astype(y_ref.dtype)

    @pl.when(ci == n_chunks - 1)
    def _():
        so_ref[...] = s_scr[...]


def _ret_kernel(p_ref, cos_ref, sin_ref, ng_ref, s0_ref, *rest, c, nb, n_chunks, n_alias):
    y_ref, so_ref, s_scr = rest[n_alias:]
    ci = pl.program_id(1)

    @pl.when(ci == 0)
    def _():
        s_scr[...] = s0_ref[...]

    qk = RET_HEADS * RET_DK
    vw = RET_HEADS * RET_DV
    half = RET_DK // 2
    row = lax.broadcasted_iota(jnp.int32, (c, c), 0)
    col = lax.broadcasted_iota(jnp.int32, (c, c), 1)
    lag = (row - col).astype(F32)
    pos = lax.broadcasted_iota(jnp.int32, (c, 1), 0).astype(F32)
    cos = cos_ref[pl.ds(pl.multiple_of(ci * c, c), c), :]
    sin = sin_ref[pl.ds(pl.multiple_of(ci * c, c), c), :]

    def rot(x):
        x1, x2 = x[:, :half], x[:, half:]
        return jnp.concatenate([x1 * cos - x2 * sin, x1 * sin + x2 * cos], axis=-1)

    def v_of(h, ro):
        return p_ref[ro:ro + c, 2 * qk + h * RET_DV:2 * qk + (h + 1) * RET_DV]

    staged = {}
    for h in range(RET_HEADS):
        log_gamma = math.log(1.0 - 2.0 ** (-5.0 - h))
        dec = jnp.exp(jnp.where(row >= col, lag * log_gamma, -jnp.inf))
        for sq in range(nb):
            ro = sq * c
            q = rot(p_ref[ro:ro + c, h * RET_DK:(h + 1) * RET_DK])
            k = rot(p_ref[ro:ro + c, qk + h * RET_DK:qk + (h + 1) * RET_DK]) * (RET_DK ** -0.5)
            s = s_scr[sq, h]
            att = (_dot_nt(q, k) * dec).astype(BF16)
            o_inter = jnp.exp((pos + 1.0) * log_gamma) * _dot(q, s)
            k_dec = k * jnp.exp((c - 1.0 - pos) * log_gamma)
            s_scr[sq, h] = math.exp(c * log_gamma) * s + _dot_tn(k_dec, v_of(h, ro))
            staged[h, sq] = (att, o_inter)

    for h in range(RET_HEADS):
        sl = slice(h * RET_DV, (h + 1) * RET_DV)
        y_rows = []
        for sq in range(nb):
            ro = sq * c
            att, o_inter = staged[h, sq]
            gate = p_ref[ro:ro + c, 2 * qk + vw + h * RET_DV:2 * qk + vw + (h + 1) * RET_DV]
            o = _dot(att, v_of(h, ro)) + o_inter

            mu = jnp.mean(o, axis=-1, keepdims=True)
            d = o - mu
            var = jnp.mean(d * d, axis=-1, keepdims=True)
            y_rows.append(d * lax.rsqrt(var + LN_EPS) * ng_ref[:, sl] * jax.nn.silu(gate))
        y_ref[:, sl] = (jnp.concatenate(y_rows, axis=0) if nb > 1 else y_rows[0]).astype(y_ref.dtype)

    @pl.when(ci == n_chunks - 1)
    def _():
        so_ref[...] = s_scr[...]


def _mamba_stack_width(c):
    return max(M_HPG * c, LANES)


def _mamba_tables(c):
    w = _mamba_stack_width(c)
    head = jnp.arange(LANES)[:, None]
    ch = jnp.arange(M_DI)[None, :]
    expand_ch = (ch // M_HEADDIM == head).astype(BF16)
    q = jnp.arange(M_GROUPS * w)[None, :]
    r = q % w
    valid = r < M_HPG * c
    expand_k = ((head == (q // w) * M_HPG + r // c) & valid).astype(BF16)
    i = jnp.arange(c)[:, None]
    key = r % c
    neg = jnp.where((i >= key) & valid, 0.0, -jnp.inf).astype(F32)
    diag = ((i == key) & valid).astype(F32)
    return expand_ch, expand_ch.T, expand_k, neg, diag


def _mamba_kernel(p_ref, cw_ref, cb_ref, dtb_ref, alog_ref, dskip_ref, ng_ref, ech_ref, echt_ref, ek_ref, neg_ref,
                  diag_ref, s0_ref, c0_ref, *rest, c, nb, n_chunks, n_alias):
    y_ref, so_ref, co_ref, s_scr, x_scr, xbc_scr = rest[n_alias:n_alias + 6]
    y_scr = rest[n_alias + 6] if nb > 1 else None
    first_row = lax.broadcasted_iota(jnp.int32, (8, 1), 0) == 0
    ci = pl.program_id(1)
    keep = M_CONV - 1
    gn = M_GROUPS * M_DSTATE
    w = _mamba_stack_width(c)
    used = M_HPG * c

    @pl.when(ci == 0)
    def _():
        for sq in range(nb):
            for grp in range(M_GROUPS):
                s_scr[sq, grp] = s0_ref[sq, grp * M_HPG:(grp + 1) * M_HPG].reshape(M_GROUP_W, M_DSTATE)
            r0, r1, r2 = c0_ref[sq, 0:1], c0_ref[sq, 1:2], c0_ref[sq, 2:3]
            x_scr[sq, 0:1, :] = cw_ref[0:1, :] * r2
            x_scr[sq, 1:2, :] = cw_ref[1:2, :] * r2 + cw_ref[0:1, :] * r1
            x_scr[sq, 2:3, :] = cw_ref[2:3, :] * r2 + cw_ref[1:2, :] * r1 + cw_ref[0:1, :] * r0

    dt = jax.nn.softplus(p_ref[:, M_DI + M_CONV_DIM:M_DI + M_CONV_DIM + LANES] + dtb_ref[...])
    g_all = _cumsum_rows(dt * -jnp.exp(alog_ref[...]), c)
    gk_all, = _dot_exact_lhs([g_all], ek_ref[...])
    blockmask = (lax.broadcasted_iota(jnp.int32, (w, M_GROUP_W), 1) // M_HEADDIM
                 == lax.broadcasted_iota(jnp.int32, (w, M_GROUP_W), 0) // c)

    for sq in range(nb):
        ro = sq * c
        for lo in range(0, M_CONV_DIM, CONV_COLS):
            cs = slice(lo, lo + CONV_COLS)
            x = p_ref[ro:ro + c, M_DI + lo:M_DI + lo + CONV_COLS]
            u = x * cw_ref[0:1, cs]
            for wi in range(1, M_CONV):
                carry = x_scr[sq, wi - 1:wi, cs]
                x_scr[sq, wi - 1:wi, cs] = u[c - 1:c]
                rolled = pltpu.roll(u, 1, 0)
                head_rows = jnp.where(first_row, carry, rolled[0:8])
                shifted = jnp.concatenate([head_rows, rolled[8:]], axis=0) if c > 8 else head_rows
                u = x * cw_ref[wi:wi + 1, cs] + shifted
            xbc_scr[:, cs] = jax.nn.silu(u + cb_ref[:, cs])

        @pl.when(ci == n_chunks - 1)
        def _():
            co_ref[sq] = p_ref[ro + c - keep:ro + c, M_DI:M_DI + M_CONV_DIM]

        g = g_all[ro:ro + c]
        g_last = g[c - 1:c]
        dt_x, eg_x, kf_x = _dot_exact_lhs([dt[ro:ro + c], jnp.exp(g), jnp.exp(g_last - g)], ech_ref[...], n_parts=2)
        egl_rows = jnp.exp(jnp.broadcast_to(g_last, (LANES, LANES)).T)
        egl_col = _dot_exact_rhs(echt_ref[...], egl_rows, n_parts=2)
        gk = gk_all[ro:ro + c]
        g_row = jnp.sum(gk * diag_ref[...], axis=0, keepdims=True)
        dec_all = jnp.exp(gk - g_row + neg_ref[...])

        atts = []
        for grp in range(M_GROUPS):
            bmat = xbc_scr[:, M_DI + grp * M_DSTATE:M_DI + (grp + 1) * M_DSTATE]
            cmat = xbc_scr[:, M_DI + gn + grp * M_DSTATE:M_DI + gn + (grp + 1) * M_DSTATE]
            pad_rows = [jnp.zeros((w - used, M_DSTATE), F32)] if w > used else []
            b_cat = jnp.concatenate([bmat] * M_HPG + pad_rows, axis=0)
            atts.append((_dot_nt(cmat, b_cat) * dec_all[:, grp * w:(grp + 1) * w]).astype(BF16))

        for grp in range(M_GROUPS):
            gsl = slice(grp * M_GROUP_W, (grp + 1) * M_GROUP_W)
            xs = xbc_scr[:, gsl]
            bmat = xbc_scr[:, M_DI + grp * M_DSTATE:M_DI + (grp + 1) * M_DSTATE]
            cmat = xbc_scr[:, M_DI + gn + grp * M_DSTATE:M_DI + gn + (grp + 1) * M_DSTATE]
            xdt = xs * dt_x[:, gsl]
            s_t = s_scr[sq, grp]
            pad_rows = [jnp.zeros((w - used, M_GROUP_W), F32)] if w > used else []
            x_cat = jnp.where(blockmask, jnp.concatenate([xdt] * M_HPG + pad_rows, axis=0), 0.0)
            o = _dot(atts[grp], x_cat) + eg_x[:, gsl] * _dot_nt(cmat, s_t)
            s_scr[sq, grp] = egl_col[gsl] * s_t + _dot_tn(xdt * kf_x[:, gsl], bmat)

            y = (o + dskip_ref[:, gsl] * xs) * jax.nn.silu(p_ref[ro:ro + c, gsl])
            ms = jnp.mean(y * y, axis=-1, keepdims=True)
            y = y * lax.rsqrt(ms + LN_EPS) * ng_ref[:, gsl]
            if nb > 1:
                y_scr[ro:ro + c, gsl] = y
            else:
                y_ref[:, gsl] = y.astype(y_ref.dtype)

    if nb > 1:
        y_ref[...] = y_scr[...].astype(y_ref.dtype)

    @pl.when(ci == n_chunks - 1)
    def _():
        for sq in range(nb):
            for grp in range(M_GROUPS):
                so_ref[sq, grp * M_HPG:(grp + 1) * M_HPG] = s_scr[sq, grp].reshape(M_HPG, M_HEADDIM, M_DSTATE)


def _rope_tables(pos):
    half = RET_DK // 2
    inv_freq = ROPE_BASE ** (-jnp.arange(half, dtype=F32) / half)
    ang = pos.astype(F32)[:, None] * inv_freq[None, :]
    return jnp.cos(ang), jnp.sin(ang)


def _row_tile(n_rows):
    for cand in range(min(MAX_ROW_TILE, n_rows) // BF16_ROWS * BF16_ROWS, 0, -BF16_ROWS):
        if n_rows % cand == 0:
            return cand
    raise ValueError(f"no row tile for {n_rows} rows")


def kernel(x_prompt, x_sample, state_hgrn, state_ret, state_ssm, state_conv, meta_tokens, ln_g, ln_b,
           ffn_w_gate, ffn_w_up, ffn_w_down, hg_lb_logits, hg_w_in, hg_norm_g, hg_w_o,
           ret_w_in, ret_norm_g, ret_w_o, m_w_in, m_conv_w, m_conv_b, m_dt_bias, m_a_log, m_d,
           m_norm_g, m_w_o):
    bp, sp, d = x_prompt.shape
    bs, ss, _ = x_sample.shape
    n_p, n_s = bp * sp, bs * ss
    n_rows = n_p + n_s + N_META
    tm = _row_tile(n_rows)
    tm_edge = math.gcd(math.gcd(n_p, n_s), 512)
    c_s = math.gcd(ss, CHUNK)
    c_m = math.gcd(N_META, CHUNK)

    def groups(c_prompt, sample_rows):
        nb_s = 1
        if ss == c_s:
            nb_s = max(1, math.gcd(bs, max(sample_rows, BF16_ROWS) // c_s))
        return {"meta": dict(row0=n_p + n_s, bsz=1, t=N_META, c=c_m, nb=1, shared_state=True),
                "prompt": dict(row0=0, bsz=bp, t=sp, c=c_prompt, nb=1, shared_state=True),
                "sample": dict(row0=n_p, bsz=bs, t=ss, c=c_s, nb=nb_s, shared_state=False)}

    bsz_of = {"prompt": bp, "sample": bs}
    ln_g4, ln_b4 = ln_g.reshape(DEPTH, 3, 1, d), ln_b.reshape(DEPTH, 3, 1, d)
    ffn_first = (ffn_w_gate, ffn_w_up, ffn_w_down, ln_g4, ln_b4)
    ffn = (ffn_w_gate[:, 1:].astype(BF16), ffn_w_up[:, 1:].astype(BF16), ffn_w_down[:, 1:].astype(BF16), ln_g4, ln_b4)

    def run_mixer(kernel_fn, name, p, grp_cfg, consts_of, states_in, layer, y_width, state_shapes, n_slots,
                  scratch_of, bufs):
        y = None
        meta_states = None
        new_bufs = {}
        for grp in ("meta", "prompt", "sample"):
            gk = grp_cfg[grp]
            if grp == "meta":
                states0 = [jnp.zeros((1, 1) + shp, F32) for shp in state_shapes]
                st_shapes = [jax.ShapeDtypeStruct((1, 1) + shp, F32) for shp in state_shapes]
                st_bufs, st_layer, g_slot = [None] * len(state_shapes), 0, 0
            else:
                states0, st_layer = (meta_states, 0) if grp == "prompt" else (states_in, layer)
                st_shapes = [jax.ShapeDtypeStruct((n_slots, bsz_of[grp]) + shp, F32) for shp in state_shapes]
                st_bufs = bufs[grp] if bufs is not None else [None] * len(state_shapes)
                g_slot = layer
            outs = _run_mixer(
                functools.partial(kernel_fn, c=gk["c"], nb=gk["nb"], n_chunks=gk["t"] // gk["c"]), name + "_" + grp,
                p, consts_of(grp, gk), states0, st_layer, [jax.ShapeDtypeStruct((n_rows, y_width), BF16)] + st_shapes,
                [y] + st_bufs, scratch_of(gk), y_width=y_width, state_shapes=state_shapes, slot=g_slot, **gk)
            y = outs[0]
            if grp == "meta":
                meta_states = list(outs[1:])
            else:
                new_bufs[grp] = list(outs[1:])
        return y, new_bufs

    hg_bufs = ret_bufs = ssm_bufs = None
    n_hg, n_ret, n_ssm = state_hgrn.shape[0], state_ret.shape[0], state_ssm.shape[0]
    hg_w_in_b, ret_w_in_b = hg_w_in.astype(BF16), ret_w_in.astype(BF16)
    m_w_in_b = jnp.pad(m_w_in, ((0, 0), (0, 0), (0, M_COLS_PAD - M_COLS))).astype(BF16)
    hg_w_o_b, ret_w_o_b, m_w_o_b = hg_w_o.astype(BF16), ret_w_o.astype(BF16), m_w_o.astype(BF16)
    state_ssm_t = jnp.swapaxes(state_ssm, 3, 4)
    h = None
    y_prompt = y_sample = None
    for i in range(DEPTH):
        if i == 0:
            h = _ffn_ln_first(x_prompt.reshape(n_p, d), x_sample.reshape(n_s, d), meta_tokens.astype(F32), ffn_first, i, 0, tm_edge)
        else:
            h = _ffn_ln(h, ffn_first, i, 0, tm)
        kind, j = i % 3, i // 3
        if kind == 0:
            p = _proj(h, hg_w_in_b, j, tm)
            shp = (HG_HEADS, HG_DK, HG_DV)
            y, hg_bufs = run_mixer(
                functools.partial(_hgrn_kernel, layer=i), "hgrn", p, groups(math.gcd(sp, CHUNK), 32),
                lambda grp, gk: [hg_lb_logits, hg_norm_g[j].reshape(1, -1)],
                [state_hgrn], j, HG_HEADS * HG_DV, [shp], n_hg,
                lambda gk: [pltpu.VMEM((gk["nb"],) + shp, F32)], hg_bufs)
            w_o = hg_w_o_b
        elif kind == 1:
            p = _proj(h, ret_w_in_b, j, tm)
            shp = (RET_HEADS, RET_DK, RET_DV)
            grp_cfg = groups(math.gcd(sp, RET_CHUNK), BF16_ROWS)
            rope = {"meta": _rope_tables(jnp.arange(N_META)), "prompt": _rope_tables(N_META + jnp.arange(sp)),
                    "sample": _rope_tables(PAST_LEN + jnp.arange(ss))}
            y, ret_bufs = run_mixer(
                _ret_kernel, "retention", p, grp_cfg,
                lambda grp, gk: [rope[grp][0], rope[grp][1], ret_norm_g[j].reshape(1, -1)],
                [state_ret], j, RET_HEADS * RET_DV, [shp], n_ret,
                lambda gk: [pltpu.VMEM((gk["nb"],) + shp, F32)], ret_bufs)
            w_o = ret_w_o_b
        else:
            p = _proj(h, m_w_in_b, j, tm)
            shp, cshp = (M_HEADS, M_HEADDIM, M_DSTATE), (M_CONV - 1, M_CONV_DIM)
            pad = LANES - M_HEADS
            consts = [m_conv_w[j], m_conv_b[j].reshape(1, -1), jnp.pad(m_dt_bias[j], (0, pad)).reshape(1, LANES),
                      jnp.pad(m_a_log[j], (0, pad)).reshape(1, LANES), jnp.repeat(m_d[j], M_HEADDIM).reshape(1, -1),
                      m_norm_g[j].reshape(1, -1)]
            tables = {}
            y, ssm_bufs = run_mixer(
                _mamba_kernel, "mamba", p, groups(math.gcd(sp, CHUNK), 32),
                lambda grp, gk: consts + list(tables.setdefault(gk["c"], _mamba_tables(gk["c"]))),
                [state_ssm_t, state_conv], j, M_DI, [shp, cshp], n_ssm,
                lambda gk: [pltpu.VMEM((gk["nb"], M_GROUPS, M_GROUP_W, M_DSTATE), F32),
                            pltpu.VMEM((gk["nb"], 8, M_CONV_DIM), F32), pltpu.VMEM((gk["c"], M_CONV_DIM), F32)]
                + ([pltpu.VMEM((gk["nb"] * gk["c"], M_DI), F32)] if gk["nb"] > 1 else []), ssm_bufs)
            w_o = m_w_o_b
        if i < DEPTH - 1:
            h = _mix_ffn_ln(h, y, w_o, j, ffn, i, tm)
        else:
            y_prompt, y_sample = _mix_ffn_ln(h, y, w_o, j, ffn, i, tm_edge, split=(n_p, n_s))

    return (y_prompt.reshape(bp, sp, d), y_sample.reshape(bs, ss, d),
            hg_bufs["prompt"][0], hg_bufs["sample"][0], ret_bufs["prompt"][0], ret_bufs["sample"][0],
            jnp.swapaxes(ssm_bufs["prompt"][0], 3, 4), jnp.swapaxes(ssm_bufs["sample"][0], 3, 4),
            ssm_bufs["prompt"][1], ssm_bufs["sample"][1])
```

```python
import functools
import math

import jax
import jax.numpy as jnp
from jax import lax
from jax.experimental import pallas as pl
from jax.experimental.pallas import tpu as pltpu

F32 = jnp.float32
BF16 = jnp.bfloat16

D_MODEL = 1024
DEPTH = 4
N_META = 16
PAST_LEN = 16384
CHUNK = 64
ALPHA = (2 * DEPTH) ** 0.25
LN_EPS = 1e-5
ROPE_BASE = 10000.0
HG_HEADS, HG_DK, HG_DV = 8, 128, 128
RET_HEADS, RET_DK, RET_DV = 4, 256, 512
RET_CHUNK = 256
HG_BLK = 8
M_DI, M_HEADDIM, M_HEADS, M_GROUPS, M_DSTATE, M_CONV = 2048, 64, 32, 8, 128, 4
M_CONV_DIM = M_DI + 2 * M_GROUPS * M_DSTATE
M_GROUP_W = M_DI // M_GROUPS
M_HPG = M_HEADS // M_GROUPS
M_COLS = M_DI + M_CONV_DIM + M_HEADS
LANES = 128
BF16_ROWS = 16
M_COLS_PAD = ((M_COLS + LANES - 1) // LANES) * LANES
CONV_COLS = 512
VMEM_LIMIT = 56 * 1024 * 1024
MAX_ROW_TILE = 528
LOG2E = 1.4426950408889634


def _params(n_grid, flags=None):
    return pltpu.CompilerParams(dimension_semantics=("arbitrary",) * n_grid, vmem_limit_bytes=VMEM_LIMIT, flags=flags)


def _resident(shape):
    return pl.BlockSpec(shape, lambda *_: (0,) * len(shape), pipeline_mode=pl.Buffered(1))


def _resident_at(shape, lead):
    return pl.BlockSpec((None,) * len(lead) + tuple(shape), lambda *_: tuple(lead) + (0,) * len(shape),
                        pipeline_mode=pl.Buffered(1))


def _dot(a, b):
    return jnp.dot(a.astype(BF16), b.astype(BF16), preferred_element_type=F32)


def _dot_nt(a, b):
    return lax.dot_general(a.astype(BF16), b.astype(BF16), (((1,), (1,)), ((), ())), preferred_element_type=F32)


def _dot_tn(a, b):
    return lax.dot_general(a.astype(BF16), b.astype(BF16), (((0,), (0,)), ((), ())), preferred_element_type=F32)


def _split(x, n_parts):
    parts = []
    r = x
    for _ in range(n_parts - 1):
        hi = r.astype(BF16).astype(F32)
        parts.append(hi)
        r = r - hi
    parts.append(r)
    return parts


def _dot_exact_rhs(a01, x, n_parts=3):
    n = x.shape[1]
    out = jnp.dot(a01, jnp.concatenate(_split(x, n_parts), axis=1).astype(BF16), preferred_element_type=F32)
    res = out[:, :n]
    for i in range(1, n_parts):
        res = res + out[:, i * n:(i + 1) * n]
    return res


def _dot_exact_lhs(xs, b01, n_parts=3):
    r = xs[0].shape[0]
    parts = [p for x in xs for p in _split(x, n_parts)]
    out = jnp.dot(jnp.concatenate(parts, axis=0).astype(BF16), b01, preferred_element_type=F32)
    res = []
    for i in range(len(xs)):
        acc = out[n_parts * i * r:(n_parts * i + 1) * r]
        for k in range(1, n_parts):
            acc = acc + out[(n_parts * i + k) * r:(n_parts * i + k + 1) * r]
        res.append(acc)
    return res


def _cumsum_rows(x, c):
    n = x.shape[0]
    row = lax.broadcasted_iota(jnp.int32, (n, n), 0)
    col = lax.broadcasted_iota(jnp.int32, (n, n), 1)
    tri = (row >= col) if n == c else ((row >= col) & (row // c == col // c))
    return _dot_exact_rhs(jnp.where(tri, 1.0, 0.0).astype(BF16), x)


def _layer_norm(y, g, b):
    mu = jnp.mean(y, axis=-1, keepdims=True)
    d = y - mu
    var = jnp.mean(d * d, axis=-1, keepdims=True)
    return d * lax.rsqrt(var + LN_EPS) * g + b


def _ffn_ln_kernel(*refs, n_src, n_dst, tiles, ff_chunk):
    srcs, (wg_ref, wu_ref, wd_ref, g_ref, b_ref) = refs[:n_src], refs[n_src:n_src + 5]
    dsts = refs[n_src + 5:n_src + 5 + n_dst]
    i = pl.program_id(0)
    if n_src == 1:
        x = srcs[0][...]
    elif n_src == 5:
        h_ref, y_ref, wo_ref, g1_ref, b1_ref = srcs
        mixed = jnp.dot(y_ref[...], wo_ref[...], preferred_element_type=F32)
        x = _layer_norm(ALPHA * h_ref[...] + mixed, g1_ref[...], b1_ref[...])
    else:
        x_scr = refs[-1]
        n_meta = srcs[2].shape[0]

        @pl.when(i < tiles[0])
        def _():
            x_scr[...] = srcs[0][...]

        @pl.when((i >= tiles[0]) & (i < tiles[0] + tiles[1]))
        def _():
            x_scr[...] = srcs[1][...]

        @pl.when(i == tiles[0] + tiles[1])
        def _():
            x_scr[0:n_meta, :] = srcs[2][...]
            if x_scr.shape[0] > n_meta:
                x_scr[n_meta:, :] = jnp.zeros((x_scr.shape[0] - n_meta, x_scr.shape[1]), F32)

        x = x_scr[...]
    xb = x.astype(BF16)
    d_ff = wg_ref.shape[1]
    acc = jnp.zeros(x.shape, F32)
    for lo in range(0, d_ff, ff_chunk):
        gate = jnp.dot(xb, wg_ref[:, lo:lo + ff_chunk].astype(BF16), preferred_element_type=F32)
        up = jnp.dot(xb, wu_ref[:, lo:lo + ff_chunk].astype(BF16), preferred_element_type=F32)
        act = (jax.nn.silu(gate) * up).astype(BF16)
        acc = acc + jnp.dot(act, wd_ref[lo:lo + ff_chunk, :].astype(BF16), preferred_element_type=F32)
    res = _layer_norm(ALPHA * x + 0.5 * acc, g_ref[...], b_ref[...])
    if n_dst == 1:
        dsts[0][...] = res
    else:
        @pl.when(i < tiles[0])
        def _():
            dsts[0][...] = res

        @pl.when(i >= tiles[0])
        def _():
            dsts[1][...] = res


def _ffn_weights(ffn, layer, which):
    wg, wu, wd, ln_g, ln_b = ffn
    d, d_ff = wg.shape[2:]
    ln_row = 0 if which == 0 else 2
    w_idx = (layer, which if wg.shape[1] > 1 else 0)
    specs = [_resident_at((d, d_ff), w_idx), _resident_at((d, d_ff), w_idx),
             _resident_at((d_ff, d), w_idx), _resident_at((1, d), (layer, ln_row)),
             _resident_at((1, d), (layer, ln_row))]
    return specs, (wg, wu, wd, ln_g, ln_b), d_ff // 11


def _ffn_ln(x, ffn, layer, which, tm):
    m, d = x.shape
    w_specs, w_args, ff_chunk = _ffn_weights(ffn, layer, which)
    return pl.pallas_call(
        functools.partial(_ffn_ln_kernel, n_src=1, n_dst=1, tiles=None, ff_chunk=ff_chunk),
        grid=(m // tm,),
        in_specs=[pl.BlockSpec((tm, d), lambda i: (i, 0))] + w_specs,
        out_specs=pl.BlockSpec((tm, d), lambda i: (i, 0)),
        out_shape=jax.ShapeDtypeStruct((m, d), F32),
        compiler_params=_params(1), name="ffn_ln",
    )(x, *w_args)


def _ffn_ln_first(x_p, x_s, x_m, ffn, layer, which, tm):
    (n_p, d), n_s, n_m = x_p.shape, x_s.shape[0], x_m.shape[0]
    tp, ts = n_p // tm, n_s // tm
    w_specs, w_args, ff_chunk = _ffn_weights(ffn, layer, which)
    return pl.pallas_call(
        functools.partial(_ffn_ln_kernel, n_src=3, n_dst=1, tiles=(tp, ts), ff_chunk=ff_chunk),
        grid=(tp + ts + 1,),
        in_specs=[pl.BlockSpec((tm, d), lambda i: (jnp.minimum(i, tp - 1), 0)),
                  pl.BlockSpec((tm, d), lambda i: (jnp.clip(i - tp, 0, ts - 1), 0)),
                  pl.BlockSpec((n_m, d), lambda i: (0, 0))] + w_specs,
        out_specs=pl.BlockSpec((tm, d), lambda i: (i, 0)),
        out_shape=jax.ShapeDtypeStruct((n_p + n_s + n_m, d), F32),
        scratch_shapes=[pltpu.VMEM((tm, d), F32)],
        compiler_params=_params(1), name="ffn_ln_first",
    )(x_p, x_s, x_m, *w_args)


def _proj_kernel(x_ref, w_ref, o_ref, *, col_chunk):
    xb = x_ref[...].astype(BF16)
    n = w_ref.shape[1]
    for lo in range(0, n, col_chunk):
        hi = min(lo + col_chunk, n)
        o_ref[:, lo:hi] = jnp.dot(xb, w_ref[:, lo:hi], preferred_element_type=F32)


def _proj(x, w, layer, tm):
    m, d = x.shape
    n = w.shape[2]
    return pl.pallas_call(
        functools.partial(_proj_kernel, col_chunk=512),
        grid=(m // tm,),
        in_specs=[pl.BlockSpec((tm, d), lambda i: (i, 0)), _resident_at((d, n), (layer,))],
        out_specs=pl.BlockSpec((tm, n), lambda i: (i, 0)),
        out_shape=jax.ShapeDtypeStruct((m, n), F32),
        compiler_params=_params(1), name="proj",
    )(x, w)


def _mix_ffn_ln(h, y, w_o, w_layer, ffn, layer, tm, split=None):
    m, d = h.shape
    k = y.shape[1]
    ln_g, ln_b = ffn[3], ffn[4]
    w_specs, w_args, ff_chunk = _ffn_weights(ffn, layer, 1)
    mix_specs = [pl.BlockSpec((tm, d), lambda i: (i, 0)), pl.BlockSpec((tm, k), lambda i: (i, 0)),
                 _resident_at((k, d), (w_layer,)), _resident_at((1, d), (layer, 1)), _resident_at((1, d), (layer, 1))]
    if split is None:
        grid, tiles = (m // tm,), None
        out_specs = pl.BlockSpec((tm, d), lambda i: (i, 0))
        out_shape = jax.ShapeDtypeStruct((m, d), F32)
    else:
        tp, ts = split[0] // tm, split[1] // tm
        grid, tiles = (tp + ts,), (tp, ts)
        out_specs = [pl.BlockSpec((tm, d), lambda i: (jnp.minimum(i, tp - 1), 0)),
                     pl.BlockSpec((tm, d), lambda i: (jnp.clip(i - tp, 0, ts - 1), 0))]
        out_shape = [jax.ShapeDtypeStruct((split[0], d), F32), jax.ShapeDtypeStruct((split[1], d), F32)]
    return pl.pallas_call(
        functools.partial(_ffn_ln_kernel, n_src=5, n_dst=1 if split is None else 2, tiles=tiles, ff_chunk=ff_chunk),
        grid=grid, in_specs=mix_specs + w_specs, out_specs=out_specs, out_shape=out_shape,
        compiler_params=_params(1), name="mix_ffn_ln",
    )(h, y, w_o, ln_g, ln_b, *w_args)


def _run_mixer(kernel_fn, name, p, consts, states0, state_layer, out_shapes, out_bufs, scratch, *, y_width,
               state_shapes, row0, bsz, t, c, nb, shared_state, slot):
    n_chunks = t // c
    assert nb == 1 or n_chunks == 1
    rows = nb * c
    assert row0 % rows == 0 and bsz % nb == 0
    blk0 = row0 // rows
    row_idx = lambda sb, ci: (blk0 + sb * n_chunks + ci, 0)
    in_specs = [pl.BlockSpec((rows, p.shape[1]), row_idx)]
    in_specs += [pl.BlockSpec(a.shape, lambda sb, ci, _n=a.ndim: (0,) * _n) for a in consts]
    for shp in state_shapes:
        zeros = (0,) * len(shp)
        in_specs.append(pl.BlockSpec((None, nb) + shp, (lambda sb, ci, _z=zeros: (state_layer, 0) + _z) if shared_state
                                     else (lambda sb, ci, _z=zeros: (state_layer, sb) + _z)))
    out_specs = [pl.BlockSpec((rows, y_width), row_idx)]
    for shp in state_shapes:
        zeros = (0,) * len(shp)
        out_specs.append(pl.BlockSpec((None, nb) + shp, lambda sb, ci, _z=zeros: (slot, sb) + _z))
    args = [p, *consts, *states0]
    n_in = len(args)
    aliases = {}
    for k, buf in enumerate(out_bufs):
        if buf is not None:
            aliases[len(args)] = k
            in_specs.append(pl.BlockSpec(memory_space=pl.ANY))
            args.append(buf)
    return pl.pallas_call(
        functools.partial(kernel_fn, n_alias=len(args) - n_in),
        grid=(bsz // nb, n_chunks), in_specs=in_specs, out_specs=out_specs, out_shape=out_shapes,
        scratch_shapes=scratch, input_output_aliases=aliases,
        compiler_params=_params(2), name=name,
    )(*args)


def _hgrn_kernel(p_ref, lbl_ref, ng_ref, s0_ref, *rest, c, nb, n_chunks, layer, n_alias):
    y_ref, so_ref, s_scr = rest[n_alias:]
    ci = pl.program_id(1)

    @pl.when(ci == 0)
    def _():
        s_scr[...] = s0_ref[...]

    fw = HG_HEADS * HG_DK
    logits = lbl_ref[...]
    e = jnp.exp(logits - jnp.max(logits, axis=0, keepdims=True))
    prob = e / jnp.sum(e, axis=0, keepdims=True)
    lb = prob[0:1] * 0.0
    run = prob[0:1]
    for r in range(1, layer + 1):
        run = run + prob[r:r + 1]
        lb = run - prob[0:1]

    z = p_ref[:, fw:2 * fw]
    ez = jnp.exp(-jnp.abs(z))
    s_big = 1.0 / (1.0 + ez)
    s_small = ez * s_big
    pos = z >= 0.0
    f = lb + (1.0 - lb) * jnp.where(pos, s_big, s_small)
    k_all = (1.0 - lb) * jnp.where(pos, s_small, s_big)
    g_all = _cumsum_rows(jnp.log(f), c)
    g2_all = g_all * LOG2E
    q_all = jax.nn.silu(p_ref[:, 0:fw])
    blk = HG_BLK
    row8 = lax.broadcasted_iota(jnp.int32, (blk, HG_DK), 0)
    causal = [jnp.where(row8 >= jj, 0.0, -jnp.inf) for jj in range(blk)]
    own_key = (lax.broadcasted_iota(jnp.int32, (blk * c, c), 0) // blk
               == lax.broadcasted_iota(jnp.int32, (blk * c, c), 1))
    if c > blk:
        n_off = sum(range(blk, c, blk))
        orow = lax.broadcasted_iota(jnp.int32, (c - blk, n_off), 0) // blk
        ocol = lax.broadcasted_iota(jnp.int32, (c - blk, n_off), 1)
        off_mask = jnp.zeros((c - blk, n_off), jnp.bool_)
        start = 0
        for b, r0 in enumerate(range(blk, c, blk)):
            off_mask = off_mask | ((orow == b) & (ocol >= start) & (ocol < start + r0))
            start += r0

    def v_of(h, ro):
        return p_ref[ro:ro + c, 2 * fw + h * HG_DV:2 * fw + (h + 1) * HG_DV]

    staged = {}
    for h in range(HG_HEADS):
        sl = slice(h * HG_DK, (h + 1) * HG_DK)
        for sq in range(nb):
            ro = sq * c
            q, k, g2 = q_all[ro:ro + c, sl], k_all[ro:ro + c, sl], g2_all[ro:ro + c, sl]
            v = v_of(h, ro)
            s = s_scr[sq, h]
            o_inter = _dot(q * jnp.exp2(g2), s)
            pieces = []
            for j in range(c):
                t0 = j // blk * blk
                diff = g2[t0:t0 + blk] - g2[j:j + 1]
                if j > t0:
                    diff = diff + causal[j - t0]
                pieces.append(q[t0:t0 + blk] * jnp.exp2(diff))
            scores = jnp.where(own_key, _dot_nt(jnp.concatenate(pieces, axis=0), k), 0.0)
            att_rows = []
            for t0 in range(0, c, blk):
                acc = scores[t0 * blk:(t0 + 1) * blk]
                for j in range(t0 + 1, t0 + blk):
                    acc = acc + scores[j * blk:(j + 1) * blk]
                att_rows.append(acc)
            att_near = (jnp.concatenate(att_rows, axis=0) if len(att_rows) > 1 else att_rows[0]).astype(BF16)
            att_far = None
            if c > blk:
                q_t = jnp.concatenate([q[r0:r0 + blk] * jnp.exp2(g2[r0:r0 + blk] - g2[r0:r0 + 1])
                                       for r0 in range(blk, c, blk)], axis=0)
                k_t = jnp.concatenate([k[0:r0] * jnp.exp2(g2[r0:r0 + 1] - g2[0:r0]) for r0 in range(blk, c, blk)], axis=0)
                att_far = jnp.where(off_mask, _dot_nt(q_t, k_t), 0.0).astype(BF16)
            g_last = g2[c - 1:c]
            k_dec = k * jnp.exp2(g_last - g2)
            dec_col = jnp.exp2(jnp.broadcast_to(g_last, (HG_DV, HG_DK)).T)
            s_scr[sq, h] = dec_col * s + _dot_tn(k_dec, v)
            staged[h, sq] = (o_inter, att_near, att_far)

    for h in range(HG_HEADS):
        sl = slice(h * HG_DK, (h + 1) * HG_DK)
        y_rows = []
        for sq in range(nb):
            ro = sq * c
            o_inter, att_near, att_far = staged[h, sq]
            v = v_of(h, ro)
            gate = p_ref[ro:ro + c, 2 * fw + HG_HEADS * HG_DV + h * HG_DV:2 * fw + HG_HEADS * HG_DV + (h + 1) * HG_DV]
            o = o_inter + _dot(att_near, v)
            if c > blk:
                v_t = jnp.concatenate([v[0:r0] for r0 in range(blk, c, blk)], axis=0)
                o = o + jnp.concatenate([jnp.zeros((blk, HG_DV), F32), _dot(att_far, v_t)], axis=0)
            ms = jnp.mean(o * o, axis=-1, keepdims=True)
            y_rows.append(o * lax.rsqrt(ms + LN_EPS) * ng_ref[:, sl] * jax.nn.silu(gate))
        y_ref[:, sl] = (jnp.concatenate(y_rows, axis=0) if nb > 1 else y_rows[0]).astype(y_ref.dtype)

    @pl.when(ci == n_chunks - 1)
    def _():
        so_ref[...] = s_scr[...]


def _ret_kernel(p_ref, cos_ref, sin_ref, ng_ref, s0_ref, *rest, c, nb, n_chunks, n_alias):
    y_ref, so_ref, s_scr = rest[n_alias:]
    ci = pl.program_id(1)

    @pl.when(ci == 0)
    def _():
        s_scr[...] = s0_ref[...]

    qk = RET_HEADS * RET_DK
    vw = RET_HEADS * RET_DV
    half = RET_DK // 2
    row = lax.broadcasted_iota(jnp.int32, (c, c), 0)
    col = lax.broadcasted_iota(jnp.int32, (c, c), 1)
    lag = (row - col).astype(F32)
    pos = lax.broadcasted_iota(jnp.int32, (c, 1), 0).astype(F32)
    cos = cos_ref[pl.ds(pl.multiple_of(ci * c, c), c), :]
    sin = sin_ref[pl.ds(pl.multiple_of(ci * c, c), c), :]

    def rot(x):
        x1, x2 = x[:, :half], x[:, half:]
        return jnp.concatenate([x1 * cos - x2 * sin, x1 * sin + x2 * cos], axis=-1)

    def v_of(h, ro):
        return p_ref[ro:ro + c, 2 * qk + h * RET_DV:2 * qk + (h + 1) * RET_DV]

    staged = {}
    for h in range(RET_HEADS):
        log_gamma = math.log(1.0 - 2.0 ** (-5.0 - h))
        dec = jnp.exp(jnp.where(row >= col, lag * log_gamma, -jnp.inf))
        for sq in range(nb):
            ro = sq * c
            q = rot(p_ref[ro:ro + c, h * RET_DK:(h + 1) * RET_DK])
            k = rot(p_ref[ro:ro + c, qk + h * RET_DK:qk + (h + 1) * RET_DK]) * (RET_DK ** -0.5)
            s = s_scr[sq, h]
            att = (_dot_nt(q, k) * dec).astype(BF16)
            o_inter = jnp.exp((pos + 1.0) * log_gamma) * _dot(q, s)
            k_dec = k * jnp.exp((c - 1.0 - pos) * log_gamma)
            s_scr[sq, h] = math.exp(c * log_gamma) * s + _dot_tn(k_dec, v_of(h, ro))
            staged[h, sq] = (att, o_inter)

    for h in range(RET_HEADS):
        sl = slice(h * RET_DV, (h + 1) * RET_DV)
        y_rows = []
        for sq in range(nb):
            ro = sq * c
            att, o_inter = staged[h, sq]
            gate = p_ref[ro:ro + c, 2 * qk + vw + h * RET_DV:2 * qk + vw + (h + 1) * RET_DV]
            o = _dot(att, v_of(h, ro)) + o_inter

            mu = jnp.mean(o, axis=-1, keepdims=True)
            d = o - mu
            var = jnp.mean(d * d, axis=-1, keepdims=True)
            y_rows.append(d * lax.rsqrt(var + LN_EPS) * ng_ref[:, sl] * jax.nn.silu(gate))
        y_ref[:, sl] = (jnp.concatenate(y_rows, axis=0) if nb > 1 else y_rows[0]).astype(y_ref.dtype)

    @pl.when(ci == n_chunks - 1)
    def _():
        so_ref[...] = s_scr[...]


def _mamba_stack_width(c):
    return max(M_HPG * c, LANES)


def _mamba_tables(c):
    w = _mamba_stack_width(c)
    head = jnp.arange(LANES)[:, None]
    ch = jnp.arange(M_DI)[None, :]
    expand_ch = (ch // M_HEADDIM == head).astype(BF16)
    q = jnp.arange(M_GROUPS * w)[None, :]
    r = q % w
    valid = r < M_HPG * c
    expand_k = ((head == (q // w) * M_HPG + r // c) & valid).astype(BF16)
    i = jnp.arange(c)[:, None]
    key = r % c
    neg = jnp.where((i >= key) & valid, 0.0, -jnp.inf).astype(F32)
    diag = ((i == key) & valid).astype(F32)
    return expand_ch, expand_ch.T, expand_k, neg, diag


def _mamba_kernel(p_ref, cw_ref, cb_ref, dtb_ref, alog_ref, dskip_ref, ng_ref, ech_ref, echt_ref, ek_ref, neg_ref,
                  diag_ref, s0_ref, c0_ref, *rest, c, nb, n_chunks, n_alias):
    y_ref, so_ref, co_ref, s_scr, x_scr, xbc_scr = rest[n_alias:n_alias + 6]
    y_scr = rest[n_alias + 6] if nb > 1 else None
    first_row = lax.broadcasted_iota(jnp.int32, (8, 1), 0) == 0
    ci = pl.program_id(1)
    keep = M_CONV - 1
    gn = M_GROUPS * M_DSTATE
    w = _mamba_stack_width(c)
    used = M_HPG * c

    @pl.when(ci == 0)
    def _():
        for sq in range(nb):
            for grp in range(M_GROUPS):
                s_scr[sq, grp] = s0_ref[sq, grp * M_HPG:(grp + 1) * M_HPG].reshape(M_GROUP_W, M_DSTATE)
            r0, r1, r2 = c0_ref[sq, 0:1], c0_ref[sq, 1:2], c0_ref[sq, 2:3]
            x_scr[sq, 0:1, :] = cw_ref[0:1, :] * r2
            x_scr[sq, 1:2, :] = cw_ref[1:2, :] * r2 + cw_ref[0:1, :] * r1
            x_scr[sq, 2:3, :] = cw_ref[2:3, :] * r2 + cw_ref[1:2, :] * r1 + cw_ref[0:1, :] * r0

    dt = jax.nn.softplus(p_ref[:, M_DI + M_CONV_DIM:M_DI + M_CONV_DIM + LANES] + dtb_ref[...])
    g_all = _cumsum_rows(dt * -jnp.exp(alog_ref[...]), c)
    gk_all, = _dot_exact_lhs([g_all], ek_ref[...])
    blockmask = (lax.broadcasted_iota(jnp.int32, (w, M_GROUP_W), 1) // M_HEADDIM
                 == lax.broadcasted_iota(jnp.int32, (w, M_GROUP_W), 0) // c)

    for sq in range(nb):
        ro = sq * c
        for lo in range(0, M_CONV_DIM, CONV_COLS):
            cs = slice(lo, lo + CONV_COLS)
            x = p_ref[ro:ro + c, M_DI + lo:M_DI + lo + CONV_COLS]
            u = x * cw_ref[0:1, cs]
            for wi in range(1, M_CONV):
                carry = x_scr[sq, wi - 1:wi, cs]
                x_scr[sq, wi - 1:wi, cs] = u[c - 1:c]
                rolled = pltpu.roll(u, 1, 0)
                head_rows = jnp.where(first_row, carry, rolled[0:8])
                shifted = jnp.concatenate([head_rows, rolled[8:]], axis=0) if c > 8 else head_rows
                u = x * cw_ref[wi:wi + 1, cs] + shifted
            xbc_scr[:, cs] = jax.nn.silu(u + cb_ref[:, cs])

        @pl.when(ci == n_chunks - 1)
        def _():
            co_ref[sq] = p_ref[ro + c - keep:ro + c, M_DI:M_DI + M_CONV_DIM]

        g = g_all[ro:ro + c]
        g_last = g[c - 1:c]
        dt_x, eg_x, kf_x = _dot_exact_lhs([dt[ro:ro + c], jnp.exp(g), jnp.exp(g_last - g)], ech_ref[...], n_parts=2)
        egl_rows = jnp.exp(jnp.broadcast_to(g_last, (LANES, LANES)).T)
        egl_col = _dot_exact_rhs(echt_ref[...], egl_rows, n_parts=2)
        gk = gk_all[ro:ro + c]
        g_row = jnp.sum(gk * diag_ref[...], axis=0, keepdims=True)
        dec_all = jnp.exp(gk - g_row + neg_ref[...])

        atts = []
        for grp in range(M_GROUPS):
            bmat = xbc_scr[:, M_DI + grp * M_DSTATE:M_DI + (grp + 1) * M_DSTATE]
            cmat = xbc_scr[:, M_DI + gn + grp * M_DSTATE:M_DI + gn + (grp + 1) * M_DSTATE]
            pad_rows = [jnp.zeros((w - used, M_DSTATE), F32)] if w > used else []
            b_cat = jnp.concatenate([bmat] * M_HPG + pad_rows, axis=0)
            atts.append((_dot_nt(cmat, b_cat) * dec_all[:, grp * w:(grp + 1) * w]).astype(BF16))

        for grp in range(M_GROUPS):
            gsl = slice(grp * M_GROUP_W, (grp + 1) * M_GROUP_W)
            xs = xbc_scr[:, gsl]
            bmat = xbc_scr[:, M_DI + grp * M_DSTATE:M_DI + (grp + 1) * M_DSTATE]
            cmat = xbc_scr[:, M_DI + gn + grp * M_DSTATE:M_DI + gn + (grp + 1) * M_DSTATE]
            xdt = xs * dt_x[:, gsl]
            s_t = s_scr[sq, grp]
            pad_rows = [jnp.zeros((w - used, M_GROUP_W), F32)] if w > used else []
            x_cat = jnp.where(blockmask, jnp.concatenate([xdt] * M_HPG + pad_rows, axis=0), 0.0)
            o = _dot(atts[grp], x_cat) + eg_x[:, gsl] * _dot_nt(cmat, s_t)
            s_scr[sq, grp] = egl_col[gsl] * s_t + _dot_tn(xdt * kf_x[:, gsl], bmat)

            y = (o + dskip_ref[:, gsl] * xs) * jax.nn.silu(p_ref[ro:ro + c, gsl])
            ms = jnp.mean(y * y, axis=-1, keepdims=True)
            y = y * lax.rsqrt(ms + LN_EPS) * ng_ref[:, gsl]
            if nb > 1:
                y_scr[ro:ro + c, gsl] = y
            else:
                y_ref[:, gsl] = y.astype(y_ref.dtype)

    if nb > 1:
        y_ref[...] = y_scr[...].astype(y_ref.dtype)

    @pl.when(ci == n_chunks - 1)
    def _():
        for sq in range(nb):
            for grp in range(M_GROUPS):
                so_ref[sq, grp * M_HPG:(grp + 1) * M_HPG] = s_scr[sq, grp].reshape(M_HPG, M_HEADDIM, M_DSTATE)


def _rope_tables(pos):
    half = RET_DK // 2
    inv_freq = ROPE_BASE ** (-jnp.arange(half, dtype=F32) / half)
    ang = pos.astype(F32)[:, None] * inv_freq[None, :]
    return jnp.cos(ang), jnp.sin(ang)


def _row_tile(n_rows):
    for cand in range(min(MAX_ROW_TILE, n_rows) // BF16_ROWS * BF16_ROWS, 0, -BF16_ROWS):
        if n_rows % cand == 0:
            return cand
    raise ValueError(f"no row tile for {n_rows} rows")


def kernel(x_prompt, x_sample, state_hgrn, state_ret, state_ssm, state_conv, meta_tokens, ln_g, ln_b,
           ffn_w_gate, ffn_w_up, ffn_w_down, hg_lb_logits, hg_w_in, hg_norm_g, hg_w_o,
           ret_w_in, ret_norm_g, ret_w_o, m_w_in, m_conv_w, m_conv_b, m_dt_bias, m_a_log, m_d,
           m_norm_g, m_w_o):
    bp, sp, d = x_prompt.shape
    bs, ss, _ = x_sample.shape
    n_p, n_s = bp * sp, bs * ss
    n_rows = n_p + n_s + N_META
    tm = _row_tile(n_rows)
    tm_edge = math.gcd(math.gcd(n_p, n_s), 512)
    c_s = math.gcd(ss, CHUNK)
    c_m = math.gcd(N_META, CHUNK)

    def groups(c_prompt, sample_rows):
        nb_s = 1
        if ss == c_s:
            nb_s = max(1, math.gcd(bs, max(sample_rows, BF16_ROWS) // c_s))
        return {"meta": dict(row0=n_p + n_s, bsz=1, t=N_META, c=c_m, nb=1, shared_state=True),
                "prompt": dict(row0=0, bsz=bp, t=sp, c=c_prompt, nb=1, shared_state=True),
                "sample": dict(row0=n_p, bsz=bs, t=ss, c=c_s, nb=nb_s, shared_state=False)}

    bsz_of = {"prompt": bp, "sample": bs}
    ln_g4, ln_b4 = ln_g.reshape(DEPTH, 3, 1, d), ln_b.reshape(DEPTH, 3, 1, d)
    ffn_first = (ffn_w_gate, ffn_w_up, ffn_w_down, ln_g4, ln_b4)
    ffn = (ffn_w_gate[:, 1:].astype(BF16), ffn_w_up[:, 1:].astype(BF16), ffn_w_down[:, 1:].astype(BF16), ln_g4, ln_b4)

    def run_mixer(kernel_fn, name, p, grp_cfg, consts_of, states_in, layer, y_width, state_shapes, n_slots,
                  scratch_of, bufs):
        y = None
        meta_states = None
        new_bufs = {}
        for grp in ("meta", "prompt", "sample"):
            gk = grp_cfg[grp]
            if grp == "meta":
                states0 = [jnp.zeros((1, 1) + shp, F32) for shp in state_shapes]
                st_shapes = [jax.ShapeDtypeStruct((1, 1) + shp, F32) for shp in state_shapes]
                st_bufs, st_layer, g_slot = [None] * len(state_shapes), 0, 0
            else:
                states0, st_layer = (meta_states, 0) if grp == "prompt" else (states_in, layer)
                st_shapes = [jax.ShapeDtypeStruct((n_slots, bsz_of[grp]) + shp, F32) for shp in state_shapes]
                st_bufs = bufs[grp] if bufs is not None else [None] * len(state_shapes)
                g_slot = layer
            outs = _run_mixer(
                functools.partial(kernel_fn, c=gk["c"], nb=gk["nb"], n_chunks=gk["t"] // gk["c"]), name + "_" + grp,
                p, consts_of(grp, gk), states0, st_layer, [jax.ShapeDtypeStruct((n_rows, y_width), BF16)] + st_shapes,
                [y] + st_bufs, scratch_of(gk), y_width=y_width, state_shapes=state_shapes, slot=g_slot, **gk)
            y = outs[0]
            if grp == "meta":
                meta_states = list(outs[1:])
            else:
                new_bufs[grp] = list(outs[1:])
        return y, new_bufs

    hg_bufs = ret_bufs = ssm_bufs = None
    n_hg, n_ret, n_ssm = state_hgrn.shape[0], state_ret.shape[0], state_ssm.shape[0]
    hg_w_in_b, ret_w_in_b = hg_w_in.astype(BF16), ret_w_in.astype(BF16)
    m_w_in_b = jnp.pad(m_w_in, ((0, 0), (0, 0), (0, M_COLS_PAD - M_COLS))).astype(BF16)
    hg_w_o_b, ret_w_o_b, m_w_o_b = hg_w_o.astype(BF16), ret_w_o.astype(BF16), m_w_o.astype(BF16)
    state_ssm_t = jnp.swapaxes(state_ssm, 3, 4)
    h = None
    y_prompt = y_sample = None
    for i in range(DEPTH):
        if i == 0:
            h = _ffn_ln_first(x_prompt.reshape(n_p, d), x_sample.reshape(n_s, d), meta_tokens.astype(F32), ffn_first, i, 0, tm_edge)
        else:
            h = _ffn_ln(h, ffn_first, i, 0, tm)
        kind, j = i % 3, i // 3
        if kind == 0:
            p = _proj(h, hg_w_in_b, j, tm)
            shp = (HG_HEADS, HG_DK, HG_DV)
            y, hg_bufs = run_mixer(
                functools.partial(_hgrn_kernel, layer=i), "hgrn", p, groups(math.gcd(sp, CHUNK), 32),
                lambda grp, gk: [hg_lb_logits, hg_norm_g[j].reshape(1, -1)],
                [state_hgrn], j, HG_HEADS * HG_DV, [shp], n_hg,
                lambda gk: [pltpu.VMEM((gk["nb"],) + shp, F32)], hg_bufs)
            w_o = hg_w_o_b
        elif kind == 1:
            p = _proj(h, ret_w_in_b, j, tm)
            shp = (RET_HEADS, RET_DK, RET_DV)
            grp_cfg = groups(math.gcd(sp, RET_CHUNK), BF16_ROWS)
            rope = {"meta": _rope_tables(jnp.arange(N_META)), "prompt": _rope_tables(N_META + jnp.arange(sp)),
                    "sample": _rope_tables(PAST_LEN + jnp.arange(ss))}
            y, ret_bufs = run_mixer(
                _ret_kernel, "retention", p, grp_cfg,
                lambda grp, gk: [rope[grp][0], rope[grp][1], ret_norm_g[j].reshape(1, -1)],
                [state_ret], j, RET_HEADS * RET_DV, [shp], n_ret,
                lambda gk: [pltpu.VMEM((gk["nb"],) + shp, F32)], ret_bufs)
            w_o = ret_w_o_b
        else:
            p = _proj(h, m_w_in_b, j, tm)
            shp, cshp = (M_HEADS, M_HEADDIM, M_DSTATE), (M_CONV - 1, M_CONV_DIM)
            pad = LANES - M_HEADS
            consts = [m_conv_w[j], m_conv_b[j].reshape(1, -1), jnp.pad(m_dt_bias[j], (0, pad)).reshape(1, LANES),
                      jnp.pad(m_a_log[j], (0, pad)).reshape(1, LANES), jnp.repeat(m_d[j], M_HEADDIM).reshape(1, -1),
                      m_norm_g[j].reshape(1, -1)]
            tables = {}
            y, ssm_bufs = run_mixer(
                _mamba_kernel, "mamba", p, groups(math.gcd(sp, CHUNK), 32),
                lambda grp, gk: consts + list(tables.setdefault(gk["c"], _mamba_tables(gk["c"]))),
                [state_ssm_t, state_conv], j, M_DI, [shp, cshp], n_ssm,
                lambda gk: [pltpu.VMEM((gk["nb"], M_GROUPS, M_GROUP_W, M_DSTATE), F32),
                            pltpu.VMEM((gk["nb"], 8, M_CONV_DIM), F32), pltpu.VMEM((gk["c"], M_CONV_DIM), F32)]
                + ([pltpu.VMEM((gk["nb"] * gk["c"], M_DI), F32)] if gk["nb"] > 1 else []), ssm_bufs)
            w_o = m_w_o_b
        if i < DEPTH - 1:
            h = _mix_ffn_ln(h, y, w_o, j, ffn, i, tm)
        else:
            y_prompt, y_sample = _mix_ffn_ln(h, y, w_o, j, ffn, i, tm_edge, split=(n_p, n_s))

    return (y_prompt.reshape(bp, sp, d), y_sample.reshape(bs, ss, d),
            hg_bufs["prompt"][0], hg_bufs["sample"][0], ret_bufs["prompt"][0], ret_bufs["sample"][0],
            jnp.swapaxes(ssm_bufs["prompt"][0], 3, 4), jnp.swapaxes(ssm_bufs["sample"][0], 3, 4),
            ssm_bufs["prompt"][1], ssm_bufs["sample"][1])
```
